```python
import jax
import jax.numpy as jnp
from jax import lax
import numpy as np

D_MODEL = 1024
BATCH = 4
SEQ = 4096
DEPTH = 4

GRID_W = 64
MEM_LEN = 256
HEAD_DIM = 64
Q_BLOCK = 128
ROPE_THETA = 10000.0
LN_EPS = 1e-5
RMS_EPS = 1e-6
NA_HEADS = 4
NA_ROWS = 8
NA_COLS = 16
GQA_HEADS = 8
GQA_KV_HEADS = 2
GQA_GROUP = GQA_HEADS // GQA_KV_HEADS
MLA_HEADS = 4
MLA_Q_RANK = 256
MLA_KV_RANK = 128
MLA_NOPE = 64
MLA_ROPE = 32
MLA_V = 64
N_BRANCH = 3
A_WIDTH = NA_HEADS * HEAD_DIM
B_WIDTH = GQA_HEADS * HEAD_DIM
C_WIDTH = MLA_HEADS * MLA_V
IN_SPLITS = (A_WIDTH, A_WIDTH, A_WIDTH,
             B_WIDTH, GQA_KV_HEADS * HEAD_DIM, GQA_KV_HEADS * HEAD_DIM,
             MLA_Q_RANK, MLA_KV_RANK, MLA_ROPE,
             N_BRANCH * D_MODEL)
IN_COLS = sum(IN_SPLITS)
MEM_HEADS = 4
MEM_HEAD_DIM = D_MODEL // MEM_HEADS
D_FF = ((8 * D_MODEL // 3 + 127) // 128) * 128
N_EXPERTS = 8
TOP_K = 2
N_DENSE = (DEPTH + 1) // 2
N_MOE = DEPTH // 2
ALPHA = (2 * DEPTH) ** 0.25
BETA = (8 * DEPTH) ** -0.25

kernel_name = 'hybrid_natten_gqa_mla_moe_encoder'


def layer_norm(x, g, b):
    xf = x.astype(jnp.float32)
    mu = jnp.mean(xf, axis=-1, keepdims=True)
    var = jnp.mean(jnp.square(xf - mu), axis=-1, keepdims=True)
    return ((xf - mu) * lax.rsqrt(var + LN_EPS) * g + b).astype(x.dtype)


def rms_norm(x, g):
    xf = x.astype(jnp.float32)
    return (xf * lax.rsqrt(jnp.mean(xf * xf, axis=-1, keepdims=True) + RMS_EPS) * g).astype(x.dtype)


def axial_rope_angles(seq, rot_dim):
    t = jnp.arange(seq)
    row = (t // GRID_W).astype(jnp.float32)
    col = (t % GRID_W).astype(jnp.float32)
    n = rot_dim // 4
    inv_freq = ROPE_THETA ** (-jnp.arange(n, dtype=jnp.float32) / n)
    ang = jnp.concatenate([row[:, None] * inv_freq, col[:, None] * inv_freq], axis=-1)
    return jnp.cos(ang), jnp.sin(ang)


def apply_rope(x, cos, sin):
    half = x.shape[-1] // 2
    xf = x.astype(jnp.float32)
    x1, x2 = xf[..., :half], xf[..., half:]
    c, s = cos[None, :, None, :], sin[None, :, None, :]
    return jnp.concatenate([x1 * c - x2 * s, x2 * c + x1 * s], axis=-1).astype(x.dtype)


def blocked_attention(q, k, v, scale):
    b, s, hkv, g, dk = q.shape
    nb = s // Q_BLOCK
    qb = q.reshape(b, nb, Q_BLOCK, hkv, g, dk).swapaxes(0, 1)

    def attend(q_blk):
        sc = jnp.einsum('bqhgd,bkhd->bhgqk', q_blk, k).astype(jnp.float32) * scale
        p = jax.nn.softmax(sc, axis=-1).astype(v.dtype)
        return jnp.einsum('bhgqk,bkhd->bqhgd', p, v)

    o = lax.map(attend, qb)
    return o.swapaxes(0, 1).reshape(b, s, hkv * g * v.shape[-1])


def neighborhood_attention(q, k, v, rpb):
    b, s, h, d = q.shape
    rows = s // GRID_W
    kr = min(NA_ROWS, rows)
    n_keys = kr * NA_COLS
    r = jnp.arange(rows)
    c = jnp.arange(GRID_W)
    key_r = jnp.clip(r - kr // 2, 0, rows - kr)[:, None] + jnp.arange(kr)
    key_c = jnp.clip(c - NA_COLS // 2, 0, GRID_W - NA_COLS)[:, None] + jnp.arange(NA_COLS)
    idx = (key_r[:, None, :, None] * GRID_W + key_c[None, :, None, :]).reshape(rows, GRID_W, n_keys)
    dr = key_r - r[:, None] + (NA_ROWS - 1)
    dc = key_c - c[:, None] + (NA_COLS - 1)
    bias = rpb[:, dr[:, None, :, None], dc[None, :, None, :]]
    bias = bias.reshape(h, rows, GRID_W, n_keys).transpose(1, 0, 2, 3)
    q_rows = q.reshape(b, rows, GRID_W, h, d).swapaxes(0, 1)
    scale = d ** -0.5

    def attend_row(args):
        q_r, idx_r, bias_r = args
        k_g = k[:, idx_r]
        v_g = v[:, idx_r]
        sc = jnp.einsum('bwhd,bwnhd->bhwn', q_r, k_g).astype(jnp.float32) * scale + bias_r.astype(jnp.float32)
        p = jax.nn.softmax(sc, axis=-1).astype(v.dtype)
        return jnp.einsum('bhwn,bwnhd->bwhd', p, v_g)

    o = lax.map(attend_row, (q_rows, idx, bias))
    return o.swapaxes(0, 1).reshape(b, s, h * d)


def hybrid_mixer(h, w_in, na_rpb, gqa_q_norm, gqa_k_norm, mla_q_norm, mla_kv_norm,
                 mla_w_uq, mla_w_ukv, w_proj_a, w_proj_b, w_proj_c, w_out, rope_b, rope_c):
    b, s, _ = h.shape
    z = h @ w_in
    (a_q, a_k, a_v, b_q, b_k, b_v, c_dq, c_dkv, c_kr, gate_logits) = jnp.split(
        z, np.cumsum(IN_SPLITS)[:-1].tolist(), axis=-1)

    ya = neighborhood_attention(a_q.reshape(b, s, NA_HEADS, HEAD_DIM),
                                a_k.reshape(b, s, NA_HEADS, HEAD_DIM),
                                a_v.reshape(b, s, NA_HEADS, HEAD_DIM), na_rpb)

    cos_b, sin_b = rope_b
    qb = apply_rope(rms_norm(b_q.reshape(b, s, GQA_HEADS, HEAD_DIM), gqa_q_norm), cos_b, sin_b)
    kb = apply_rope(rms_norm(b_k.reshape(b, s, GQA_KV_HEADS, HEAD_DIM), gqa_k_norm), cos_b, sin_b)
    vb = b_v.reshape(b, s, GQA_KV_HEADS, HEAD_DIM)
    yb = blocked_attention(qb.reshape(b, s, GQA_KV_HEADS, GQA_GROUP, HEAD_DIM), kb, vb, HEAD_DIM ** -0.5)

    cos_c, sin_c = rope_c
    qc = (rms_norm(c_dq, mla_q_norm) @ mla_w_uq).reshape(b, s, MLA_HEADS, MLA_NOPE + MLA_ROPE)
    q_nope, q_pe = qc[..., :MLA_NOPE], apply_rope(qc[..., MLA_NOPE:], cos_c, sin_c)
    kvc = (rms_norm(c_dkv, mla_kv_norm) @ mla_w_ukv).reshape(b, s, MLA_HEADS, MLA_NOPE + MLA_V)
    k_nope, vc = kvc[..., :MLA_NOPE], kvc[..., MLA_NOPE:]
    k_pe = apply_rope(c_kr[:, :, None, :], cos_c, sin_c)
    q_full = jnp.concatenate([q_nope, q_pe], axis=-1)[:, :, :, None, :]
    k_full = jnp.concatenate([k_nope, jnp.broadcast_to(k_pe, (b, s, MLA_HEADS, MLA_ROPE))], axis=-1)
    yc = blocked_attention(q_full, k_full, vc, (MLA_NOPE + MLA_ROPE) ** -0.5)

    gates = jax.nn.sigmoid(gate_logits.astype(jnp.float32)).astype(h.dtype).reshape(b, s, N_BRANCH, D_MODEL)
    merged = (gates[:, :, 0] * (ya @ w_proj_a)
              + gates[:, :, 1] * (yb @ w_proj_b)
              + gates[:, :, 2] * (yc @ w_proj_c))
    return merged @ w_out


def memory_cross_attention(h, mem, wq, wkv, wo):
    b, s, _ = h.shape
    m = mem.shape[1]
    q = (h @ wq).reshape(b, s, MEM_HEADS, MEM_HEAD_DIM)
    kv = (mem @ wkv).reshape(b, m, 2, MEM_HEADS, MEM_HEAD_DIM)
    k, v = kv[:, :, 0], kv[:, :, 1]
    sc = jnp.einsum('bqhd,bkhd->bhqk', q, k).astype(jnp.float32) * (MEM_HEAD_DIM ** -0.5)
    p = jax.nn.softmax(sc, axis=-1).astype(v.dtype)
    o = jnp.einsum('bhqk,bkhd->bqhd', p, v).reshape(b, s, D_MODEL)
    return o @ wo


def swiglu(t, w_gu, w_down):
    a, g = jnp.split(t @ w_gu, 2, axis=-1)
    return (jax.nn.silu(a) * g) @ w_down


def moe_swiglu(h, w_router, w_gu, w_down):
    b, s, d = h.shape
    t = h.reshape(b * s, d)
    logits = (t @ w_router).astype(jnp.float32)
    top_logit, top_idx = lax.top_k(logits, TOP_K)
    top_w = jax.nn.softmax(top_logit, axis=-1)
    gate = jnp.einsum('tk,tke->te', top_w,
                      jax.nn.one_hot(top_idx, N_EXPERTS, dtype=jnp.float32)).astype(h.dtype)
    out = jnp.zeros_like(t)
    for e in range(N_EXPERTS):
        out = out + gate[:, e:e + 1] * swiglu(t, w_gu[e], w_down[e])
    return out.reshape(b, s, d)


def setup_inputs(seed: int = 0) -> dict:
    key = jax.random.key(seed)
    k = jax.random.split(key, 26)

    def nrm(kk, shape, std):
        return jax.random.normal(kk, shape, jnp.float32) * std

    def gain(kk, shape):
        return 1.0 + 0.05 * jax.random.normal(kk, shape, jnp.float32)

    return {
        'x': nrm(k[0], (BATCH, SEQ, D_MODEL), 1.0),
        'mem': nrm(k[1], (BATCH, MEM_LEN, D_MODEL), 1.0),
        'emb_ln_g': gain(k[2], (D_MODEL,)),
        'emb_ln_b': nrm(k[3], (D_MODEL,), 0.02),
        'w_in': nrm(k[4], (DEPTH, D_MODEL, IN_COLS), D_MODEL ** -0.5),
        'na_rpb': nrm(k[5], (DEPTH, NA_HEADS, 2 * NA_ROWS - 1, 2 * NA_COLS - 1), 0.1),
        'gqa_q_norm': gain(k[6], (DEPTH, HEAD_DIM)),
        'gqa_k_norm': gain(k[7], (DEPTH, HEAD_DIM)),
        'mla_q_norm': gain(k[8], (DEPTH, MLA_Q_RANK)),
        'mla_kv_norm': gain(k[9], (DEPTH, MLA_KV_RANK)),
        'mla_w_uq': nrm(k[10], (DEPTH, MLA_Q_RANK, MLA_HEADS * (MLA_NOPE + MLA_ROPE)), MLA_Q_RANK ** -0.5),
        'mla_w_ukv': nrm(k[11], (DEPTH, MLA_KV_RANK, MLA_HEADS * (MLA_NOPE + MLA_V)), MLA_KV_RANK ** -0.5),
        'w_proj_a': nrm(k[12], (DEPTH, A_WIDTH, D_MODEL), A_WIDTH ** -0.5),
        'w_proj_b': nrm(k[13], (DEPTH, B_WIDTH, D_MODEL), B_WIDTH ** -0.5),
        'w_proj_c': nrm(k[14], (DEPTH, C_WIDTH, D_MODEL), C_WIDTH ** -0.5),
        'w_out': nrm(k[15], (DEPTH, D_MODEL, D_MODEL), BETA * D_MODEL ** -0.5),
        'mem_wq': nrm(k[16], (DEPTH, D_MODEL, D_MODEL), D_MODEL ** -0.5),
        'mem_wkv': nrm(k[17], (DEPTH, D_MODEL, 2 * D_MODEL), D_MODEL ** -0.5),
        'mem_wo': nrm(k[18], (DEPTH, D_MODEL, D_MODEL), BETA * D_MODEL ** -0.5),
        'ln_g': gain(k[19], (DEPTH, 3, D_MODEL)),
        'ln_b': nrm(k[20], (DEPTH, 3, D_MODEL), 0.02),
        'ffn_w_gu': nrm(k[21], (N_DENSE, D_MODEL, 2 * D_FF), D_MODEL ** -0.5),
        'ffn_w_down': nrm(k[22], (N_DENSE, D_FF, D_MODEL), BETA * D_FF ** -0.5),
        'moe_router': nrm(k[23], (N_MOE, D_MODEL, N_EXPERTS), D_MODEL ** -0.5),
        'moe_w_gu': nrm(k[24], (N_MOE, N_EXPERTS, D_MODEL, 2 * D_FF), D_MODEL ** -0.5),
        'moe_w_down': nrm(k[25], (N_MOE, N_EXPERTS, D_FF, D_MODEL), BETA * D_FF ** -0.5),
    }


def reference(x, mem, emb_ln_g, emb_ln_b, w_in, na_rpb, gqa_q_norm, gqa_k_norm, mla_q_norm,
              mla_kv_norm, mla_w_uq, mla_w_ukv, w_proj_a, w_proj_b, w_proj_c, w_out,
              mem_wq, mem_wkv, mem_wo, ln_g, ln_b, ffn_w_gu, ffn_w_down,
              moe_router, moe_w_gu, moe_w_down):
    s = x.shape[1]
    rope_b = axial_rope_angles(s, HEAD_DIM)
    rope_c = axial_rope_angles(s, MLA_ROPE)
    h = layer_norm(x, emb_ln_g, emb_ln_b)
    for l in range(DEPTH):
        mix = hybrid_mixer(h, w_in[l], na_rpb[l], gqa_q_norm[l], gqa_k_norm[l], mla_q_norm[l],
                           mla_kv_norm[l], mla_w_uq[l], mla_w_ukv[l], w_proj_a[l], w_proj_b[l],
                           w_proj_c[l], w_out[l], rope_b, rope_c)
        h = layer_norm(ALPHA * h + mix, ln_g[l, 0], ln_b[l, 0])
        xa = memory_cross_attention(h, mem, mem_wq[l], mem_wkv[l], mem_wo[l])
        h = layer_norm(ALPHA * h + xa, ln_g[l, 1], ln_b[l, 1])
        if l % 2 == 0:
            f = swiglu(h, ffn_w_gu[l // 2], ffn_w_down[l // 2])
        else:
            f = moe_swiglu(h, moe_router[l // 2], moe_w_gu[l // 2], moe_w_down[l // 2])
        h = layer_norm(ALPHA * h + f, ln_g[l, 2], ln_b[l, 2])
    return h
```

```python
import functools

import jax
import jax.numpy as jnp
import numpy as np
from jax import lax
from jax.experimental import pallas as pl
from jax.experimental.pallas import tpu as pltpu

F32 = jnp.float32
BF16 = jnp.bfloat16

D_MODEL = 1024
DEPTH = 4
GRID_W = 64
HEAD_DIM = 64
ROPE_THETA = 10000.0
LN_EPS = 1e-5
RMS_EPS = 1e-6
NA_HEADS = 4
NA_ROWS = 8
NA_COLS = 16
GQA_HEADS = 8
GQA_KV_HEADS = 2
MLA_HEADS = 4
MLA_Q_RANK = 256
MLA_KV_RANK = 128
MLA_NOPE = 64
MLA_ROPE = 32
MLA_V = 64
A_WIDTH = NA_HEADS * HEAD_DIM
B_WIDTH = GQA_HEADS * HEAD_DIM
C_WIDTH = MLA_HEADS * MLA_V
IN_SPLITS = (A_WIDTH, A_WIDTH, A_WIDTH,
             B_WIDTH, GQA_KV_HEADS * HEAD_DIM, GQA_KV_HEADS * HEAD_DIM,
             MLA_Q_RANK, MLA_KV_RANK, MLA_ROPE,
             3 * D_MODEL)
IN_OFFS = tuple(int(v) for v in np.cumsum((0,) + IN_SPLITS))
MEM_HEADS = 4
MEM_HEAD_DIM = D_MODEL // MEM_HEADS
D_FF = ((8 * D_MODEL // 3 + 127) // 128) * 128
N_EXPERTS = 8
ALPHA = (2 * DEPTH) ** 0.25

LANES = 128
FF_CHUNK = 256
N_FF_CHUNKS = D_FF // FF_CHUNK
ROW_TILE = 512
ATT_TQ = 512
ATT_TK = 512
NA_ROWS_PER_STEP = 8
NEG_BIG = -1e30
VMEM_LIMIT = 56 * 1024 * 1024


def _params(*sem):
    return pltpu.CompilerParams(dimension_semantics=sem, vmem_limit_bytes=VMEM_LIMIT)


def _dot(a, b):
    return jnp.dot(a, b, preferred_element_type=F32)


def _dot_nt(a, b):
    return lax.dot_general(a, b, (((1,), (1,)), ((), ())), preferred_element_type=F32)


def _ln_rows(y, g, b):
    mu = jnp.mean(y, axis=-1, keepdims=True)
    d = y - mu
    var = jnp.mean(d * d, axis=-1, keepdims=True)
    return d * lax.rsqrt(var + LN_EPS) * g + b


def _row_spec(tm, n):
    return pl.BlockSpec((tm, n), lambda i: (i, 0))


def _full_spec(shape):
    nd = len(shape)
    return pl.BlockSpec(shape, lambda *_: (0,) * nd)


def _ln_kernel(x_ref, g_ref, b_ref, o_ref):
    o_ref[...] = _ln_rows(x_ref[...], g_ref[...], b_ref[...])


def _layer_norm(x, g, b):
    t, d = x.shape
    return pl.pallas_call(
        _ln_kernel, grid=(t // ROW_TILE,),
        in_specs=[_row_spec(ROW_TILE, d), _full_spec((1, d)), _full_spec((1, d))],
        out_specs=_row_spec(ROW_TILE, d),
        out_shape=jax.ShapeDtypeStruct((t, d), F32),
        compiler_params=_params("parallel"), name="embed_ln",
    )(x, g.reshape(1, d), b.reshape(1, d))


def _proj_kernel(x_ref, w_ref, o_ref, *, act, n_chunk):
    x = x_ref[...].astype(BF16)
    n = w_ref.shape[1]
    for c in range(0, n, n_chunk):
        e = min(c + n_chunk, n)
        y = _dot(x, w_ref[:, c:e])
        if act == "sigmoid":
            y = jax.nn.sigmoid(y)
        o_ref[:, c:e] = y.astype(o_ref.dtype)


def _proj(x, w, act=None, tm=ROW_TILE, n_chunk=512, name="proj"):
    t, k = x.shape
    n = w.shape[1]
    n_chunk = min(n_chunk, n)
    return pl.pallas_call(
        functools.partial(_proj_kernel, act=act, n_chunk=n_chunk), grid=(t // tm,),
        in_specs=[_row_spec(tm, k), _full_spec((k, n))],
        out_specs=_row_spec(tm, n),
        out_shape=jax.ShapeDtypeStruct((t, n), BF16),
        compiler_params=_params("parallel"), name=name,
    )(x, w)


def _bpro_kernel(x_ref, w_ref, gq_ref, gqr_ref, gk_ref, gkr_ref, cos_ref, sin_ref, seg_ref,
                 q_ref, k_ref, v_ref):
    x = x_ref[...].astype(BF16)
    cos = cos_ref[...]
    sin = sin_ref[...]
    seg = seg_ref[...]

    def norm_rope(z, zr, g, gr):
        ms = _dot((z * z).astype(BF16), seg)
        return (z * g * cos + zr * gr * sin) * lax.rsqrt(ms + RMS_EPS)

    nq = B_WIDTH
    z = _dot(x, w_ref[:, :2 * nq])
    for j in range(nq // LANES):
        lo, hi = j * LANES, (j + 1) * LANES
        q = norm_rope(z[:, lo:hi], z[:, nq + lo:nq + hi], gq_ref[...], gqr_ref[...])
        q_ref[:, lo:hi] = (q * (HEAD_DIM ** -0.5)).astype(BF16)
    zk = _dot(x, w_ref[:, 2 * nq:])
    k_ref[...] = norm_rope(zk[:, :LANES], zk[:, LANES:2 * LANES], gk_ref[...], gkr_ref[...]).astype(BF16)
    v_ref[...] = zk[:, 2 * LANES:].astype(BF16)


def _b_prologue(h, w, gq, gqr, gk, gkr, cos, sin, seg, seq):
    t, d = h.shape
    tm = ROW_TILE
    nsb = seq // tm
    tab = pl.BlockSpec((tm, LANES), lambda i: (i % nsb, 0))
    vec = _full_spec((1, LANES))
    return pl.pallas_call(
        _bpro_kernel, grid=(t // tm,),
        in_specs=[_row_spec(tm, d), _full_spec(w.shape), vec, vec, vec, vec, tab, tab,
                  _full_spec((LANES, LANES))],
        out_specs=[_row_spec(tm, B_WIDTH), _row_spec(tm, LANES), _row_spec(tm, LANES)],
        out_shape=[jax.ShapeDtypeStruct((t, B_WIDTH), BF16),
                   jax.ShapeDtypeStruct((t, LANES), BF16),
                   jax.ShapeDtypeStruct((t, LANES), BF16)],
        compiler_params=_params("parallel"), name="b_prologue",
    )(h, w, gq, gqr, gk, gkr, cos, sin, seg)


def _cpro_kernel(x_ref, w_ref, gq_ref, gkv_ref, wuq_ref, wukv_ref, cos_ref, sin_ref,
                 q_ref, k_ref, v_ref):
    x = x_ref[...].astype(BF16)
    cos = cos_ref[...]
    sin = sin_ref[...]
    z = _dot(x, w_ref[...])
    nh = MLA_HEADS
    scale = (MLA_NOPE + MLA_ROPE) ** -0.5

    def rms(v, g):
        return (v * lax.rsqrt(jnp.mean(v * v, axis=-1, keepdims=True) + RMS_EPS) * g).astype(BF16)

    qq = _dot(rms(z[:, :MLA_Q_RANK], gq_ref[...]), wuq_ref[...])
    for hd in range(nh):
        lo, hi = hd * LANES, (hd + 1) * LANES
        q = qq[:, lo:hi] * cos + qq[:, nh * LANES + lo:nh * LANES + hi] * sin
        q_ref[:, lo:hi] = (q * scale).astype(BF16)
    o = MLA_Q_RANK + MLA_KV_RANK
    kk = _dot(rms(z[:, MLA_Q_RANK:o], gkv_ref[...]), wukv_ref[...])
    kpe = z[:, o:o + LANES] * cos + z[:, o + LANES:o + 2 * LANES] * sin
    for hd in range(nh):
        lo, hi = hd * LANES, (hd + 1) * LANES
        k_ref[:, lo:hi] = (kk[:, lo:hi] + kpe).astype(BF16)
    v_ref[...] = kk[:, nh * LANES:].astype(BF16)


def _c_prologue(h, w, gq, gkv, wuq, wukv, cos, sin, seq):
    t, d = h.shape
    tm = ROW_TILE
    nsb = seq // tm
    tab = pl.BlockSpec((tm, LANES), lambda i: (i % nsb, 0))
    nq = MLA_HEADS * LANES
    return pl.pallas_call(
        _cpro_kernel, grid=(t // tm,),
        in_specs=[_row_spec(tm, d), _full_spec(w.shape), _full_spec(gq.shape), _full_spec(gkv.shape),
                  _full_spec(wuq.shape), _full_spec(wukv.shape), tab, tab],
        out_specs=[_row_spec(tm, nq), _row_spec(tm, nq), _row_spec(tm, C_WIDTH)],
        out_shape=[jax.ShapeDtypeStruct((t, nq), BF16),
                   jax.ShapeDtypeStruct((t, nq), BF16),
                   jax.ShapeDtypeStruct((t, C_WIDTH), BF16)],
        compiler_params=_params("parallel"), name="c_prologue",
    )(h, w, gq, gkv, wuq, wukv, cos, sin)


def _attn_kernel(q_ref, k_ref, v_ref, o_ref, m_ref, l_ref, acc_ref, *, shared, tk):
    tq = o_ref.shape[0]
    seq = k_ref.shape[0]
    left = lax.broadcasted_iota(jnp.int32, (tq, LANES), 1) < HEAD_DIM
    if shared:
        q2 = q_ref[...]
        zero = jnp.zeros_like(q2)
        qs = (jnp.where(left, q2, zero), jnp.where(left, zero, q2))
    else:
        qs = (q_ref[:, :LANES], q_ref[:, LANES:])
    m_ref[...] = jnp.full(m_ref.shape, NEG_BIG, F32)
    l_ref[...] = jnp.zeros(l_ref.shape, F32)
    acc_ref[...] = jnp.zeros(acc_ref.shape, F32)

    def body(c, carry):
        off = pl.multiple_of(c * tk, tk)
        vc = v_ref[pl.ds(off, tk), :]
        for a in range(2):
            if shared:
                kc = k_ref[pl.ds(off, tk), :]
            else:
                kc = k_ref[pl.ds(off, tk), a * LANES:(a + 1) * LANES]
            s = _dot_nt(qs[a], kc)
            m_prev = m_ref[a]
            m_new = jnp.maximum(m_prev, jnp.max(s, axis=-1, keepdims=True))
            alpha = jnp.exp(m_prev - m_new)
            p = jnp.exp(s - m_new)
            l_ref[a] = alpha * l_ref[a] + jnp.sum(p, axis=-1, keepdims=True)
            acc_ref[a] = alpha * acc_ref[a] + _dot(p.astype(BF16), vc)
            m_ref[a] = m_new
        return carry

    lax.fori_loop(0, seq // tk, body, 0)
    o0 = acc_ref[0] / l_ref[0]
    o1 = acc_ref[1] / l_ref[1]
    o_ref[...] = jnp.where(left, o0, o1).astype(o_ref.dtype)


def _paired_attention(q, k, v, batch, seq, shared):
    t = q.shape[0]
    n_pairs = v.shape[1] // LANES if not shared else q.shape[1] // LANES
    tq, tk = ATT_TQ, ATT_TK
    nqb = seq // tq
    qw = LANES if shared else 2 * LANES
    q_spec = pl.BlockSpec((tq, qw), lambda b, j, i: (b * nqb + i, j))
    if shared:
        k_spec = pl.BlockSpec((seq, LANES), lambda b, j, i: (b, 0))
        v_spec = pl.BlockSpec((seq, LANES), lambda b, j, i: (b, 0))
    else:
        k_spec = pl.BlockSpec((seq, 2 * LANES), lambda b, j, i: (b, j))
        v_spec = pl.BlockSpec((seq, LANES), lambda b, j, i: (b, j))
    return pl.pallas_call(
        functools.partial(_attn_kernel, shared=shared, tk=tk),
        grid=(batch, n_pairs, nqb),
        in_specs=[q_spec, k_spec, v_spec],
        out_specs=pl.BlockSpec((tq, LANES), lambda b, j, i: (b * nqb + i, j)),
        out_shape=jax.ShapeDtypeStruct((t, n_pairs * LANES), BF16),
        scratch_shapes=[pltpu.VMEM((2, tq, 1), F32), pltpu.VMEM((2, tq, 1), F32),
                        pltpu.VMEM((2, tq, LANES), F32)],
        compiler_params=_params("parallel", "parallel", "arbitrary"),
        name="attn_shared" if shared else "attn_split",
    )(q, k, v)


def _na_kernel(q_ref, k_ref, v_ref, bias_ref, o_ref, *, rows_per_step):
    seq = k_ref.shape[0]
    rows = seq // GRID_W
    n_win = NA_ROWS * GRID_W
    left = lax.broadcasted_iota(jnp.int32, (GRID_W, LANES), 1) < HEAD_DIM

    def row_body(i, carry):
        r = pl.program_id(1) * rows_per_step + i
        r0 = jnp.clip(r - NA_ROWS // 2, 0, rows - NA_ROWS)
        rho = r - r0
        qoff = pl.multiple_of(i * GRID_W, GRID_W)
        koff = pl.multiple_of(r0 * GRID_W, GRID_W)
        outs = []
        for j in range(A_WIDTH // LANES):
            qj = q_ref[pl.ds(qoff, GRID_W), j * LANES:(j + 1) * LANES]
            kj = k_ref[pl.ds(koff, n_win), j * LANES:(j + 1) * LANES]
            vj = v_ref[pl.ds(koff, n_win), j * LANES:(j + 1) * LANES]
            zero = jnp.zeros_like(qj)
            halves = []
            for a in range(2):
                qm = jnp.where(left, qj, zero) if a == 0 else jnp.where(left, zero, qj)
                s = _dot_nt(qm, kj) + bias_ref[2 * j + a, rho]
                p = jnp.exp(s - jnp.max(s, axis=-1, keepdims=True))
                inv = 1.0 / jnp.sum(p, axis=-1, keepdims=True)
                halves.append(_dot(p.astype(BF16), vj) * inv)
            outs.append(jnp.where(left, halves[0], halves[1]))
        o_ref[pl.ds(qoff, GRID_W), :] = jnp.concatenate(outs, axis=-1).astype(o_ref.dtype)
        return carry

    lax.fori_loop(0, rows_per_step, row_body, 0)


def _neighborhood_attention(qkv, bias, batch, seq):
    t = qkv.shape[0]
    rows = seq // GRID_W
    rps = NA_ROWS_PER_STEP
    tq = rps * GRID_W
    nqb = rows // rps
    return pl.pallas_call(
        functools.partial(_na_kernel, rows_per_step=rps),
        grid=(batch, nqb),
        in_specs=[pl.BlockSpec((tq, A_WIDTH), lambda b, i: (b * nqb + i, 0)),
                  pl.BlockSpec((seq, A_WIDTH), lambda b, i: (b, 1)),
                  pl.BlockSpec((seq, A_WIDTH), lambda b, i: (b, 2)),
                  _full_spec(bias.shape)],
        out_specs=pl.BlockSpec((tq, A_WIDTH), lambda b, i: (b * nqb + i, 0)),
        out_shape=jax.ShapeDtypeStruct((t, A_WIDTH), BF16),
        compiler_params=_params("parallel", "arbitrary"), name="nbr_attn",
    )(qkv, qkv, qkv, bias)


def _na_bias_tiles(rpb, rows):
    c = np.arange(GRID_W)
    c0 = np.clip(c - NA_COLS // 2, 0, GRID_W - NA_COLS)
    kc = np.arange(GRID_W)
    in_win = (kc[None, :] >= c0[:, None]) & (kc[None, :] < c0[:, None] + NA_COLS)
    dc = np.clip(kc[None, :] - c[:, None] + (NA_COLS - 1), 0, 2 * NA_COLS - 2)
    kr = min(NA_ROWS, rows)
    rho = np.arange(kr)
    dr = np.arange(kr)[None, :] - rho[:, None] + (NA_ROWS - 1)
    tiles = rpb[:, dr[:, :, None, None], dc[None, None, :, :]]
    tiles = jnp.where(in_win[None, None, None], tiles, NEG_BIG)
    tiles = tiles.transpose(0, 1, 3, 2, 4).reshape(rpb.shape[0], kr, GRID_W, kr * GRID_W)
    return tiles.astype(F32)


def _merge_kernel(ya_ref, yb_ref, yc_ref, gate_ref, h_ref, wa_ref, wb_ref, wc_ref, wo_ref,
                  g_ref, b_ref, o_ref):
    d = D_MODEL
    merged = (gate_ref[:, :d].astype(F32) * _dot(ya_ref[...], wa_ref[...])
              + gate_ref[:, d:2 * d].astype(F32) * _dot(yb_ref[...], wb_ref[...])
              + gate_ref[:, 2 * d:].astype(F32) * _dot(yc_ref[...], wc_ref[...]))
    mix = _dot(merged.astype(BF16), wo_ref[...])
    o_ref[...] = _ln_rows(ALPHA * h_ref[...] + mix, g_ref[...], b_ref[...])


def _merge_out(ya, yb, yc, gates, h, wa, wb, wc, wo, g, b):
    t, d = h.shape
    tm = ROW_TILE
    return pl.pallas_call(
        _merge_kernel, grid=(t // tm,),
        in_specs=[_row_spec(tm, ya.shape[1]), _row_spec(tm, yb.shape[1]), _row_spec(tm, yc.shape[1]),
                  _row_spec(tm, 3 * d), _row_spec(tm, d),
                  _full_spec(wa.shape), _full_spec(wb.shape), _full_spec(wc.shape), _full_spec(wo.shape),
                  _full_spec((1, d)), _full_spec((1, d))],
        out_specs=_row_spec(tm, d),
        out_shape=jax.ShapeDtypeStruct((t, d), F32),
        compiler_params=_params("parallel"), name="merge_out",
    )(ya, yb, yc, gates, h, wa, wb, wc, wo, g, b)


def _cross_kernel(h_ref, k_ref, v_ref, wq_ref, wo_ref, g_ref, b_ref, o_ref):
    h = h_ref[...]
    q = (_dot(h.astype(BF16), wq_ref[...]) * (MEM_HEAD_DIM ** -0.5)).astype(BF16)
    outs = []
    for hd in range(MEM_HEADS):
        lo, hi = hd * MEM_HEAD_DIM, (hd + 1) * MEM_HEAD_DIM
        s = _dot_nt(q[:, lo:hi], k_ref[:, lo:hi])
        p = jnp.exp(s - jnp.max(s, axis=-1, keepdims=True))
        inv = 1.0 / jnp.sum(p, axis=-1, keepdims=True)
        outs.append((_dot(p.astype(BF16), v_ref[:, lo:hi]) * inv).astype(BF16))
    xa = _dot(jnp.concatenate(outs, axis=-1), wo_ref[...])
    o_ref[...] = _ln_rows(ALPHA * h + xa, g_ref[...], b_ref[...])


def _cross_attention(h, kv, wq, wo, g, b, batch, seq):
    t, d = h.shape
    tm = ROW_TILE
    nsb = seq // tm
    m = kv.shape[0] // batch
    return pl.pallas_call(
        _cross_kernel, grid=(batch, nsb),
        in_specs=[pl.BlockSpec((tm, d), lambda bi, i: (bi * nsb + i, 0)),
                  pl.BlockSpec((m, d), lambda bi, i: (bi, 0)),
                  pl.BlockSpec((m, d), lambda bi, i: (bi, 1)),
                  _full_spec(wq.shape), _full_spec(wo.shape), _full_spec((1, d)), _full_spec((1, d))],
        out_specs=pl.BlockSpec((tm, d), lambda bi, i: (bi * nsb + i, 0)),
        out_shape=jax.ShapeDtypeStruct((t, d), F32),
        compiler_params=_params("parallel", "arbitrary"), name="cross_attn",
    )(h, kv, kv, wq, wo, g, b)


def _swiglu_acc(x, wg_ref, wu_ref, wd_ref, acc_ref):
    acc_ref[...] = jnp.zeros(acc_ref.shape, F32)

    def body(c, carry):
        a = _dot(x, wg_ref[0, c])
        u = _dot(x, wu_ref[0, c])
        act = (a * jax.nn.sigmoid(a) * u).astype(BF16)
        acc_ref[...] += _dot(act, wd_ref[0, c])
        return carry

    lax.fori_loop(0, wg_ref.shape[1], body, 0)


def _ffn_dense_kernel(h_ref, wg_ref, wu_ref, wd_ref, g_ref, b_ref, o_ref, acc_ref):
    h = h_ref[...]
    _swiglu_acc(h.astype(BF16), wg_ref, wu_ref, wd_ref, acc_ref)
    o_ref[...] = _ln_rows(ALPHA * h + acc_ref[...], g_ref[...], b_ref[...])


def _ffn_dense(h, wg, wu, wd, g, b):
    t, d = h.shape
    tm = ROW_TILE
    wspec = lambda w: pl.BlockSpec((1,) + w.shape[1:], lambda i: (0, 0, 0, 0))
    return pl.pallas_call(
        _ffn_dense_kernel, grid=(t // tm,),
        in_specs=[_row_spec(tm, d), wspec(wg), wspec(wu), wspec(wd), _full_spec((1, d)), _full_spec((1, d))],
        out_specs=_row_spec(tm, d),
        out_shape=jax.ShapeDtypeStruct((t, d), F32),
        scratch_shapes=[pltpu.VMEM((tm, d), F32)],
        compiler_params=_params("parallel"), name="ffn_dense",
    )(h, wg, wu, wd, g, b)


def _ffn_grouped_kernel(eid_ref, nvalid_ref, x_ref, wg_ref, wu_ref, wd_ref, o_ref, acc_ref):
    i = pl.program_id(0)

    @pl.when(i < nvalid_ref[0])
    def _():
        _swiglu_acc(x_ref[...].astype(BF16), wg_ref, wu_ref, wd_ref, acc_ref)
        o_ref[...] = acc_ref[...]

    @pl.when(i >= nvalid_ref[0])
    def _():
        o_ref[...] = jnp.zeros(o_ref.shape, F32)


def _ffn_grouped(xs, tile_eid, n_valid, wg, wu, wd):
    p, d = xs.shape
    tm = ROW_TILE
    wspec = lambda w: pl.BlockSpec((1,) + w.shape[1:], lambda i, eid, nv: (eid[i], 0, 0, 0))
    grid_spec = pltpu.PrefetchScalarGridSpec(
        num_scalar_prefetch=2, grid=(p // tm,),
        in_specs=[pl.BlockSpec((tm, d), lambda i, eid, nv: (i, 0)), wspec(wg), wspec(wu), wspec(wd)],
        out_specs=pl.BlockSpec((tm, d), lambda i, eid, nv: (i, 0)),
        scratch_shapes=[pltpu.VMEM((tm, d), F32)])
    return pl.pallas_call(
        _ffn_grouped_kernel, grid_spec=grid_spec,
        out_shape=jax.ShapeDtypeStruct((p, d), F32),
        compiler_params=_params("arbitrary"), name="ffn_grouped",
    )(tile_eid, n_valid, xs, wg, wu, wd)


def _router_kernel(h_ref, wh_ref, wl_ref, tri_ref, info_ref, cnt_ref, carry_ref):
    i = pl.program_id(0)

    @pl.when(i == 0)
    def _():
        carry_ref[...] = jnp.zeros(carry_ref.shape, F32)

    x = h_ref[...]
    xh = x.astype(BF16)
    xl = (x - xh.astype(F32)).astype(BF16)
    logits = _dot(xh, wh_ref[...]) + (_dot(xh, wl_ref[...]) + _dot(xl, wh_ref[...]))
    tm = x.shape[0]
    lane = lax.broadcasted_iota(jnp.int32, (tm, LANES), 1).astype(F32)
    lg = jnp.where(lane < N_EXPERTS, logits, NEG_BIG)
    m1 = jnp.max(lg, axis=-1, keepdims=True)
    i1 = jnp.min(jnp.where(lg == m1, lane, float(LANES)), axis=-1, keepdims=True)
    lg2 = jnp.where(lane == i1, NEG_BIG, lg)
    m2 = jnp.max(lg2, axis=-1, keepdims=True)
    i2 = jnp.min(jnp.where(lg2 == m2, lane, float(LANES)), axis=-1, keepdims=True)
    e = jnp.exp(m2 - m1)
    w1 = 1.0 / (1.0 + e)
    w2 = e * w1
    hot = jnp.logical_or(lane == i1, lane == i2).astype(F32)
    before = _dot(tri_ref[...], hot.astype(BF16)) + carry_ref[...]
    r1 = jnp.sum(jnp.where(lane == i1, before, 0.0), axis=-1, keepdims=True)
    r2 = jnp.sum(jnp.where(lane == i2, before, 0.0), axis=-1, keepdims=True)
    carry_ref[...] += jnp.sum(hot, axis=0, keepdims=True)
    cols = (i1, i2, w1, w2, r1, r2)
    info = jnp.zeros((tm, LANES), F32)
    for n, col in enumerate(cols):
        info = jnp.where(lane == n, col, info)
    info_ref[...] = info
    cnt_ref[...] = carry_ref[...]


def _router(h, w_router):
    t, d = h.shape
    tm = ROW_TILE
    wpad = jnp.zeros((d, LANES), F32).at[:, :N_EXPERTS].set(w_router)
    wh = wpad.astype(BF16)
    wl = (wpad - wh.astype(F32)).astype(BF16)
    tri = (np.arange(tm)[:, None] > np.arange(tm)[None, :]).astype(np.float32)
    return pl.pallas_call(
        _router_kernel, grid=(t // tm,),
        in_specs=[_row_spec(tm, d), _full_spec((d, LANES)), _full_spec((d, LANES)), _full_spec((tm, tm))],
        out_specs=[_row_spec(tm, LANES), _full_spec((1, LANES))],
        out_shape=[jax.ShapeDtypeStruct((t, LANES), F32), jax.ShapeDtypeStruct((1, LANES), F32)],
        scratch_shapes=[pltpu.VMEM((1, LANES), F32)],
        compiler_params=_params("arbitrary"), name="router",
    )(h, wh, wl, jnp.asarray(tri, BF16))


def _dispatch_kernel(pos1_ref, pos2_ref, x_ref, init_ref, xs_ref, sem):
    del init_ref
    tm = x_ref.shape[0]
    base = pl.program_id(0) * tm

    def copy(t, pos_ref):
        return pltpu.make_async_copy(x_ref.at[pl.ds(t, 1)], xs_ref.at[pl.ds(pos_ref[base + t], 1)], sem)

    def start(t, carry):
        copy(t, pos1_ref).start()
        copy(t, pos2_ref).start()
        return carry

    def wait(t, carry):
        copy(t, pos1_ref).wait()
        copy(t, pos2_ref).wait()
        return carry

    lax.fori_loop(0, tm, start, 0)
    lax.fori_loop(0, tm, wait, 0)


def _dispatch(h, pos1, pos2, n_slots):
    t, d = h.shape
    tm = ROW_TILE
    grid_spec = pltpu.PrefetchScalarGridSpec(
        num_scalar_prefetch=2, grid=(t // tm,),
        in_specs=[pl.BlockSpec((tm, d), lambda i, p1, p2: (i, 0)), pl.BlockSpec(memory_space=pl.ANY)],
        out_specs=pl.BlockSpec(memory_space=pl.ANY),
        scratch_shapes=[pltpu.SemaphoreType.DMA(())])
    return pl.pallas_call(
        _dispatch_kernel, grid_spec=grid_spec,
        out_shape=jax.ShapeDtypeStruct((n_slots, d), F32),
        input_output_aliases={3: 0},
        compiler_params=_params("arbitrary"), name="moe_dispatch",
    )(pos1, pos2, h, jnp.zeros((n_slots, d), F32))


def _combine_kernel(pos1_ref, pos2_ref, h_ref, w1_ref, w2_ref, ys_ref, g_ref, b_ref, o_ref, buf_ref, sem):
    tm = h_ref.shape[0]
    base = pl.program_id(0) * tm

    def copy(t, k, pos_ref):
        return pltpu.make_async_copy(ys_ref.at[pl.ds(pos_ref[base + t], 1)], buf_ref.at[k, pl.ds(t, 1)], sem)

    def start(t, carry):
        copy(t, 0, pos1_ref).start()
        copy(t, 1, pos2_ref).start()
        return carry

    def wait(t, carry):
        copy(t, 0, pos1_ref).wait()
        copy(t, 1, pos2_ref).wait()
        return carry

    lax.fori_loop(0, tm, start, 0)
    lax.fori_loop(0, tm, wait, 0)
    f = w1_ref[...] * buf_ref[0] + w2_ref[...] * buf_ref[1]
    o_ref[...] = _ln_rows(ALPHA * h_ref[...] + f, g_ref[...], b_ref[...])


def _combine(h, ys, pos1, pos2, w1, w2, g, b):
    t, d = h.shape
    tm = ROW_TILE
    row = lambda n: pl.BlockSpec((tm, n), lambda i, p1, p2: (i, 0))
    full = lambda s: pl.BlockSpec(s, lambda i, p1, p2: (0, 0))
    grid_spec = pltpu.PrefetchScalarGridSpec(
        num_scalar_prefetch=2, grid=(t // tm,),
        in_specs=[row(d), row(1), row(1), pl.BlockSpec(memory_space=pl.ANY), full((1, d)), full((1, d))],
        out_specs=row(d),
        scratch_shapes=[pltpu.VMEM((2, tm, d), F32), pltpu.SemaphoreType.DMA(())])
    return pl.pallas_call(
        _combine_kernel, grid_spec=grid_spec,
        out_shape=jax.ShapeDtypeStruct((t, d), F32),
        compiler_params=_params("arbitrary"), name="moe_combine",
    )(pos1, pos2, h, w1, w2, ys, g, b)


def _moe_layer(h, w_router, wg, wu, wd, g, b):
    t, d = h.shape
    tm = ROW_TILE
    info, cnt = _router(h, w_router)
    idx1 = info[:, 0].astype(jnp.int32)
    idx2 = info[:, 1].astype(jnp.int32)
    counts = cnt[0, :N_EXPERTS].astype(jnp.int32)
    padded = ((counts + tm - 1) // tm) * tm
    ends = jnp.cumsum(padded)
    offs = ends - padded
    pos1 = offs[idx1] + info[:, 4].astype(jnp.int32)
    pos2 = offs[idx2] + info[:, 5].astype(jnp.int32)
    n_tiles = (2 * t) // tm + N_EXPERTS
    tile_eid = jnp.minimum(
        jnp.searchsorted(ends, jnp.arange(n_tiles, dtype=jnp.int32) * tm, side="right"),
        N_EXPERTS - 1).astype(jnp.int32)
    n_valid = (ends[-1:] // tm).astype(jnp.int32)
    xs = _dispatch(h, pos1, pos2, n_tiles * tm)
    ys = _ffn_grouped(xs, tile_eid, n_valid, wg, wu, wd)
    return _combine(h, ys, pos1, pos2, info[:, 2:3], info[:, 3:4], g, b)


def _rope_tables(seq):
    tpos = jnp.arange(seq)
    row = (tpos // GRID_W).astype(F32)
    col = (tpos % GRID_W).astype(F32)

    def angles(rot_dim):
        n = rot_dim // 4
        inv = ROPE_THETA ** (-jnp.arange(n, dtype=F32) / n)
        ang = jnp.concatenate([row[:, None] * inv, col[:, None] * inv], axis=-1)
        return jnp.cos(ang), jnp.sin(ang)

    cb, sb = angles(HEAD_DIM)
    cos_b = jnp.concatenate([cb, cb, cb, cb], axis=-1)
    sin_b = jnp.concatenate([-sb, sb, -sb, sb], axis=-1)
    cc, sc = angles(MLA_ROPE)
    ones = jnp.ones((seq, MLA_NOPE), F32)
    z64 = jnp.zeros((seq, MLA_NOPE), F32)
    z32 = jnp.zeros((seq, LANES - MLA_NOPE - MLA_ROPE), F32)
    cos_c = jnp.concatenate([ones, cc, cc, z32], axis=-1)
    sin_c = jnp.concatenate([z64, -sc, sc, z32], axis=-1)
    return cos_b, sin_b, cos_c, sin_c


def _chunk_cols(w):
    lead = w.shape[:-2]
    k = w.shape[-2]
    w = w.reshape(lead + (k, N_FF_CHUNKS, FF_CHUNK))
    return jnp.swapaxes(w, -3, -2).astype(BF16)


def _ffn_weights(w_gu, w_down):
    wg = _chunk_cols(w_gu[..., :D_FF])
    wu = _chunk_cols(w_gu[..., D_FF:])
    wd = w_down.reshape(w_down.shape[0], N_FF_CHUNKS, FF_CHUNK, w_down.shape[-1]).astype(BF16)
    return wg, wu, wd


def _mixer_weights(w_in, gq, gk, w_uq, w_ukv, w_proj_b):
    o = IN_OFFS
    d = w_in.shape[0]
    half = HEAD_DIM // 2
    w_a = jnp.concatenate([w_in[:, o[0]:o[1]] * (HEAD_DIM ** -0.5), w_in[:, o[1]:o[3]]], axis=1)
    pair_order = np.array([0, 4, 1, 5, 2, 6, 3, 7])
    wq = w_in[:, o[3]:o[4]].reshape(d, GQA_HEADS, HEAD_DIM)[:, pair_order]
    wk = w_in[:, o[4]:o[5]].reshape(d, GQA_KV_HEADS, HEAD_DIM)
    w_b = jnp.concatenate([wq.reshape(d, -1), jnp.roll(wq, half, axis=-1).reshape(d, -1),
                           wk.reshape(d, -1), jnp.roll(wk, half, axis=-1).reshape(d, -1),
                           w_in[:, o[5]:o[6]]], axis=1)
    g2 = lambda g: jnp.tile(g, LANES // HEAD_DIM).reshape(1, LANES)
    gains_b = (g2(gq), g2(jnp.roll(gq, half)), g2(gk), g2(jnp.roll(gk, half)))
    wkr = w_in[:, o[8]:o[9]]
    pad_kr = lambda w: jnp.concatenate(
        [jnp.zeros((d, MLA_NOPE), F32), w, jnp.zeros((d, LANES - MLA_NOPE - MLA_ROPE), F32)], axis=1)
    w_c = jnp.concatenate([w_in[:, o[6]:o[8]], pad_kr(wkr), pad_kr(jnp.roll(wkr, MLA_ROPE // 2, axis=-1))],
                          axis=1)
    uq = w_uq.reshape(MLA_Q_RANK, MLA_HEADS, MLA_NOPE + MLA_ROPE)
    zq = lambda n: jnp.zeros((MLA_Q_RANK, MLA_HEADS, n), F32)
    q_pad = jnp.concatenate([uq, zq(LANES - MLA_NOPE - MLA_ROPE)], axis=-1)
    q_rot = jnp.concatenate([zq(MLA_NOPE), jnp.roll(uq[..., MLA_NOPE:], MLA_ROPE // 2, axis=-1),
                             zq(LANES - MLA_NOPE - MLA_ROPE)], axis=-1)
    wuq = jnp.concatenate([q_pad.reshape(MLA_Q_RANK, -1), q_rot.reshape(MLA_Q_RANK, -1)], axis=1)
    ukv = w_ukv.reshape(MLA_KV_RANK, MLA_HEADS, MLA_NOPE + MLA_V)
    k_part = jnp.concatenate([ukv[..., :MLA_NOPE], jnp.zeros((MLA_KV_RANK, MLA_HEADS, LANES - MLA_NOPE), F32)],
                             axis=-1)
    wukv = jnp.concatenate([k_part.reshape(MLA_KV_RANK, -1), ukv[..., MLA_NOPE:].reshape(MLA_KV_RANK, -1)],
                           axis=1)
    wpb = w_proj_b.reshape(GQA_HEADS, HEAD_DIM, -1)[pair_order].reshape(B_WIDTH, -1)
    bf = lambda w: w.astype(BF16)
    return bf(w_a), bf(w_b), gains_b, bf(w_c), bf(wuq), bf(wukv), bf(w_in[:, o[9]:]), bf(wpb)


def kernel(x, mem, emb_ln_g, emb_ln_b, w_in, na_rpb, gqa_q_norm, gqa_k_norm, mla_q_norm, mla_kv_norm,
           mla_w_uq, mla_w_ukv, w_proj_a, w_proj_b, w_proj_c, w_out, mem_wq, mem_wkv, mem_wo, ln_g, ln_b,
           ffn_w_gu, ffn_w_down, moe_router, moe_w_gu, moe_w_down):
    batch, seq, d = x.shape
    t = batch * seq
    cos_b, sin_b, cos_c, sin_c = _rope_tables(seq)
    seg = np.kron(np.eye(LANES // HEAD_DIM), np.full((HEAD_DIM, HEAD_DIM), 1.0 / HEAD_DIM))
    seg = jnp.asarray(seg, BF16)
    mem2 = mem.reshape(-1, d)
    ffn_w = _ffn_weights(ffn_w_gu, ffn_w_down)
    vec = lambda v: v.reshape(1, -1)

    h = _layer_norm(x.reshape(t, d), emb_ln_g, emb_ln_b)
    for l in range(DEPTH):
        w_a, w_b, gains_b, w_c, wuq, wukv, w_g, wpb = _mixer_weights(
            w_in[l], gqa_q_norm[l], gqa_k_norm[l], mla_w_uq[l], mla_w_ukv[l], w_proj_b[l])
        qkv_a = _proj(h, w_a, name="proj_a")
        ya = _neighborhood_attention(qkv_a, _na_bias_tiles(na_rpb[l], seq // GRID_W), batch, seq)
        qb, kb, vb = _b_prologue(h, w_b, *gains_b, cos_b, sin_b, seg, seq)
        yb = _paired_attention(qb, kb, vb, batch, seq, shared=True)
        qc, kc, vc = _c_prologue(h, w_c, vec(mla_q_norm[l]), vec(mla_kv_norm[l]), wuq, wukv, cos_c, sin_c, seq)
        yc = _paired_attention(qc, kc, vc, batch, seq, shared=False)
        gates = _proj(h, w_g, act="sigmoid", name="proj_gates")
        h = _merge_out(ya, yb, yc, gates, h, w_proj_a[l].astype(BF16), wpb, w_proj_c[l].astype(BF16),
                       w_out[l].astype(BF16), vec(ln_g[l, 0]), vec(ln_b[l, 0]))
        kv = _proj(mem2, mem_wkv[l].astype(BF16), tm=256, name="proj_mem_kv")
        h = _cross_attention(h, kv, mem_wq[l].astype(BF16), mem_wo[l].astype(BF16),
                             vec(ln_g[l, 1]), vec(ln_b[l, 1]), batch, seq)
        if l % 2 == 0:
            wg, wu, wd = (w[l // 2:l // 2 + 1] for w in ffn_w)
            h = _ffn_dense(h, wg, wu, wd, vec(ln_g[l, 2]), vec(ln_b[l, 2]))
        else:
            wg, wu, wd = _ffn_weights(moe_w_gu[l // 2], moe_w_down[l // 2])
            h = _moe_layer(h, moe_router[l // 2], wg, wu, wd, vec(ln_g[l, 2]), vec(ln_b[l, 2]))
    return h.reshape(batch, seq, d)
```

```python
import functools

import jax
import jax.numpy as jnp
import numpy as np
from jax import lax
from jax.experimental import pallas as pl
from jax.experimental.pallas import tpu as pltpu

F32 = jnp.float32
BF16 = jnp.bfloat16

D_MODEL = 1024
DEPTH = 4
GRID_W = 64
HEAD_DIM = 64
ROPE_THETA = 10000.0
LN_EPS = 1e-5
RMS_EPS = 1e-6
NA_HEADS = 4
NA_ROWS = 8
NA_COLS = 16
GQA_HEADS = 8
GQA_KV_HEADS = 2
MLA_HEADS = 4
MLA_Q_RANK = 256
MLA_KV_RANK = 128
MLA_NOPE = 64
MLA_ROPE = 32
MLA_V = 64
A_WIDTH = NA_HEADS * HEAD_DIM
B_WIDTH = GQA_HEADS * HEAD_DIM
C_WIDTH = MLA_HEADS * MLA_V
IN_SPLITS = (A_WIDTH, A_WIDTH, A_WIDTH,
             B_WIDTH, GQA_KV_HEADS * HEAD_DIM, GQA_KV_HEADS * HEAD_DIM,
             MLA_Q_RANK, MLA_KV_RANK, MLA_ROPE,
             3 * D_MODEL)
IN_OFFS = tuple(int(v) for v in np.cumsum((0,) + IN_SPLITS))
MEM_HEADS = 4
MEM_HEAD_DIM = D_MODEL // MEM_HEADS
D_FF = ((8 * D_MODEL // 3 + 127) // 128) * 128
N_EXPERTS = 8
ALPHA = (2 * DEPTH) ** 0.25

LANES = 128
FF_CHUNK = 256
N_FF_CHUNKS = D_FF // FF_CHUNK
ROW_TILE = 512
ATT_TQ = 512
NA_ROWS_PER_STEP = 8
NEG_BIG = -1e30
LOG2E = 1.4426950408889634
ROW_L0 = HEAD_DIM
ROW_L1 = 0
VMEM_LIMIT = 56 * 1024 * 1024


def _params(*sem):
    return pltpu.CompilerParams(dimension_semantics=sem, vmem_limit_bytes=VMEM_LIMIT)


def _dot(a, b):
    return jnp.dot(a, b, preferred_element_type=F32)


def _dot_nt(a, b):
    return lax.dot_general(a, b, (((1,), (1,)), ((), ())), preferred_element_type=F32)


def _ln_rows(y, g, b):
    mu = jnp.mean(y, axis=-1, keepdims=True)
    d = y - mu
    var = jnp.mean(d * d, axis=-1, keepdims=True)
    return d * lax.rsqrt(var + LN_EPS) * g + b


def _row_spec(tm, n):
    return pl.BlockSpec((tm, n), lambda i: (i, 0))


def _full_spec(shape):
    nd = len(shape)
    return pl.BlockSpec(shape, lambda *_: (0,) * nd)


def _ln_kernel(x_ref, g_ref, b_ref, o_ref):
    o_ref[...] = _ln_rows(x_ref[...], g_ref[...], b_ref[...])


def _layer_norm(x, g, b):
    t, d = x.shape
    return pl.pallas_call(
        _ln_kernel, grid=(t // ROW_TILE,),
        in_specs=[_row_spec(ROW_TILE, d), _full_spec((1, d)), _full_spec((1, d))],
        out_specs=_row_spec(ROW_TILE, d),
        out_shape=jax.ShapeDtypeStruct((t, d), F32),
        compiler_params=_params("parallel"), name="embed_ln",
    )(x, g.reshape(1, d), b.reshape(1, d))


def _proj_kernel(x_ref, w_ref, o_ref, *, act, n_chunk):
    x = x_ref[...].astype(BF16)
    n = w_ref.shape[1]
    for c in range(0, n, n_chunk):
        e = min(c + n_chunk, n)
        y = _dot(x, w_ref[:, c:e])
        if act == "sigmoid":
            y = jax.nn.sigmoid(y)
        o_ref[:, c:e] = y.astype(o_ref.dtype)


def _proj(x, w, act=None, tm=ROW_TILE, n_chunk=512, name="proj"):
    t, k = x.shape
    n = w.shape[1]
    n_chunk = min(n_chunk, n)
    return pl.pallas_call(
        functools.partial(_proj_kernel, act=act, n_chunk=n_chunk), grid=(t // tm,),
        in_specs=[_row_spec(tm, k), _full_spec((k, n))],
        out_specs=_row_spec(tm, n),
        out_shape=jax.ShapeDtypeStruct((t, n), BF16),
        compiler_params=_params("parallel"), name=name,
    )(x, w)


def _bpro_kernel(x_ref, w_ref, gq_ref, gqr_ref, gk_ref, gkr_ref, cos_ref, sin_ref, seg_ref,
                 q_ref, k_ref, v_ref):
    x = x_ref[...].astype(BF16)
    cos = cos_ref[...]
    sin = sin_ref[...]
    seg = seg_ref[...]

    def norm_rope(z, zr, g, gr):
        ms = _dot((z * z).astype(BF16), seg)
        return (z * g * cos + zr * gr * sin) * lax.rsqrt(ms + RMS_EPS)

    nq = B_WIDTH
    z = _dot(x, w_ref[:, :2 * nq])
    for j in range(nq // LANES):
        lo, hi = j * LANES, (j + 1) * LANES
        q = norm_rope(z[:, lo:hi], z[:, nq + lo:nq + hi], gq_ref[...], gqr_ref[...])
        q_ref[:, lo:hi] = (q * (HEAD_DIM ** -0.5 * LOG2E)).astype(BF16)
    zk = _dot(x, w_ref[:, 2 * nq:])
    k_ref[...] = norm_rope(zk[:, :LANES], zk[:, LANES:2 * LANES], gk_ref[...], gkr_ref[...]).astype(BF16)
    v_ref[0] = _value_rows(zk[:, 2 * LANES:])


def _b_prologue(h, w, gq, gqr, gk, gkr, cos, sin, seg, seq):
    t, d = h.shape
    tm = ROW_TILE
    nsb = seq // tm
    tab = pl.BlockSpec((tm, LANES), lambda i: (i % nsb, 0))
    vec = _full_spec((1, LANES))
    return pl.pallas_call(
        _bpro_kernel, grid=(t // tm,),
        in_specs=[_row_spec(tm, d), _full_spec(w.shape), vec, vec, vec, vec, tab, tab,
                  _full_spec((LANES, LANES))],
        out_specs=[_row_spec(tm, B_WIDTH), _row_spec(tm, LANES),
                   pl.BlockSpec((1, 2 * LANES, tm), lambda i: (i, 0, 0))],
        out_shape=[jax.ShapeDtypeStruct((t, B_WIDTH), BF16),
                   jax.ShapeDtypeStruct((t, LANES), BF16),
                   jax.ShapeDtypeStruct((t // tm, 2 * LANES, tm), BF16)],
        compiler_params=_params("parallel"), name="b_prologue",
    )(h, w, gq, gqr, gk, gkr, cos, sin, seg)


def _cpro_kernel(x_ref, w_ref, gq_ref, gkv_ref, wuq_ref, wukv_ref, cos_ref, sin_ref,
                 q_ref, k_ref, v_ref):
    x = x_ref[...].astype(BF16)
    cos = cos_ref[...]
    sin = sin_ref[...]
    z = _dot(x, w_ref[...])
    nh = MLA_HEADS
    scale = (MLA_NOPE + MLA_ROPE) ** -0.5 * LOG2E

    def rms(v, g):
        return (v * lax.rsqrt(jnp.mean(v * v, axis=-1, keepdims=True) + RMS_EPS) * g).astype(BF16)

    qq = _dot(rms(z[:, :MLA_Q_RANK], gq_ref[...]), wuq_ref[...])
    for hd in range(nh):
        lo, hi = hd * LANES, (hd + 1) * LANES
        q = qq[:, lo:hi] * cos + qq[:, nh * LANES + lo:nh * LANES + hi] * sin
        q_ref[:, lo:hi] = (q * scale).astype(BF16)
    o = MLA_Q_RANK + MLA_KV_RANK
    kk = _dot(rms(z[:, MLA_Q_RANK:o], gkv_ref[...]), wukv_ref[...])
    kpe = z[:, o:o + LANES] * cos + z[:, o + LANES:o + 2 * LANES] * sin
    for hd in range(nh):
        lo, hi = hd * LANES, (hd + 1) * LANES
        k_ref[:, lo:hi] = (kk[:, lo:hi] + kpe).astype(BF16)
    for j in range(C_WIDTH // LANES):
        vj = kk[:, (nh + j) * LANES:(nh + j + 1) * LANES]
        v_ref[0, 2 * j * LANES:2 * (j + 1) * LANES, :] = _value_rows(vj)


def _c_prologue(h, w, gq, gkv, wuq, wukv, cos, sin, seq):
    t, d = h.shape
    tm = ROW_TILE
    nsb = seq // tm
    tab = pl.BlockSpec((tm, LANES), lambda i: (i % nsb, 0))
    nq = MLA_HEADS * LANES
    return pl.pallas_call(
        _cpro_kernel, grid=(t // tm,),
        in_specs=[_row_spec(tm, d), _full_spec(w.shape), _full_spec(gq.shape), _full_spec(gkv.shape),
                  _full_spec(wuq.shape), _full_spec(wukv.shape), tab, tab],
        out_specs=[_row_spec(tm, nq), _row_spec(tm, nq),
                   pl.BlockSpec((1, 2 * C_WIDTH, tm), lambda i: (i, 0, 0))],
        out_shape=[jax.ShapeDtypeStruct((t, nq), BF16),
                   jax.ShapeDtypeStruct((t, nq), BF16),
                   jax.ShapeDtypeStruct((t // tm, 2 * C_WIDTH, tm), BF16)],
        compiler_params=_params("parallel"), name="c_prologue",
    )(h, w, gq, gkv, wuq, wukv, cos, sin)


def _attn_kernel(q_ref, k_ref, vt_ref, o_ref, m_ref, acc_ref, st0_ref, st1_ref, *, shared):
    tq = o_ref.shape[0]
    n_chunks, _, tk = vt_ref.shape
    if shared:
        left = lax.broadcasted_iota(jnp.int32, (tq, LANES), 1) < HEAD_DIM
        q2 = q_ref[...]
        zero = jnp.zeros_like(q2)
        qs = (jnp.where(left, q2, zero), jnp.where(left, zero, q2))
    else:
        qs = (q_ref[:, :LANES], q_ref[:, LANES:])
    m_ref[...] = jnp.full(m_ref.shape, NEG_BIG, F32)
    acc_ref[...] = jnp.zeros(acc_ref.shape, F32)

    def scores(c, st_ref):
        off = pl.multiple_of(c * tk, tk)
        for a in range(2):
            if shared:
                kc = k_ref[pl.ds(off, tk), :]
            else:
                kc = k_ref[pl.ds(off, tk), a * LANES:(a + 1) * LANES]
            st_ref[a] = _dot_nt(kc, qs[a])

    def consume(c, st_ref):
        for a in range(2):
            st = st_ref[a]
            m_prev = m_ref[a]
            m_new = jnp.maximum(m_prev, jnp.max(st, axis=0, keepdims=True))
            alpha = jnp.exp2(m_prev - m_new)
            p = jnp.exp2(st - m_new).astype(BF16)
            acc_ref[a] = alpha * acc_ref[a] + _dot(vt_ref[c, a * LANES:(a + 1) * LANES, :], p)
            m_ref[a] = m_new

    bufs = (st0_ref, st1_ref)
    scores(0, st0_ref)
    for c in range(n_chunks):
        if c + 1 < n_chunks:
            scores(c + 1, bufs[(c + 1) % 2])
        consume(c, bufs[c % 2])
    top = lax.broadcasted_iota(jnp.int32, (LANES, tq), 0) < HEAD_DIM
    acc0 = acc_ref[0]
    acc1 = acc_ref[1]
    ot = jnp.where(top, acc0 / acc0[ROW_L0:ROW_L0 + 1], acc1 / acc1[ROW_L1:ROW_L1 + 1])
    o_ref[...] = ot.T.astype(o_ref.dtype)


def _value_rows(v):
    vt = v.T
    row = lax.broadcasted_iota(jnp.int32, vt.shape, 0)
    v0 = jnp.where(row < HEAD_DIM, vt, jnp.where(row == ROW_L0, 1.0, 0.0))
    v1 = jnp.where(row >= HEAD_DIM, vt, jnp.where(row == ROW_L1, 1.0, 0.0))
    return jnp.concatenate([v0, v1], axis=0).astype(BF16)


def _paired_attention(q, k, vt, batch, seq, shared):
    t = q.shape[0]
    tk = vt.shape[2]
    n_pairs = vt.shape[1] // (2 * LANES) if not shared else q.shape[1] // LANES
    tq = ATT_TQ
    nqb = seq // tq
    nkc = seq // tk
    qw = LANES if shared else 2 * LANES
    q_spec = pl.BlockSpec((tq, qw), lambda b, j, i: (b * nqb + i, j))
    if shared:
        k_spec = pl.BlockSpec((seq, LANES), lambda b, j, i: (b, 0))
        v_spec = pl.BlockSpec((nkc, 2 * LANES, tk), lambda b, j, i: (b, 0, 0))
    else:
        k_spec = pl.BlockSpec((seq, 2 * LANES), lambda b, j, i: (b, j))
        v_spec = pl.BlockSpec((nkc, 2 * LANES, tk), lambda b, j, i: (b, j, 0))
    return pl.pallas_call(
        functools.partial(_attn_kernel, shared=shared),
        grid=(batch, n_pairs, nqb),
        in_specs=[q_spec, k_spec, v_spec],
        out_specs=pl.BlockSpec((tq, LANES), lambda b, j, i: (b * nqb + i, j)),
        out_shape=jax.ShapeDtypeStruct((t, n_pairs * LANES), BF16),
        scratch_shapes=[pltpu.VMEM((2, 1, tq), F32), pltpu.VMEM((2, LANES, tq), F32),
                        pltpu.VMEM((2, tk, tq), F32), pltpu.VMEM((2, tk, tq), F32)],
        compiler_params=_params("parallel", "parallel", "arbitrary"),
        name="attn_shared" if shared else "attn_split",
    )(q, k, vt)


def _na_kernel(q_ref, k_ref, v_ref, bias_ref, o_ref, *, rows_per_step):
    seq = k_ref.shape[0]
    rows = seq // GRID_W
    n_win = NA_ROWS * GRID_W
    left = lax.broadcasted_iota(jnp.int32, (GRID_W, LANES), 1) < HEAD_DIM

    def row_body(i, carry):
        r = pl.program_id(1) * rows_per_step + i
        r0 = jnp.clip(r - NA_ROWS // 2, 0, rows - NA_ROWS)
        rho = r - r0
        qoff = pl.multiple_of(i * GRID_W, GRID_W)
        koff = pl.multiple_of(r0 * GRID_W, GRID_W)
        outs = []
        for j in range(A_WIDTH // LANES):
            qj = q_ref[pl.ds(qoff, GRID_W), j * LANES:(j + 1) * LANES]
            kj = k_ref[pl.ds(koff, n_win), j * LANES:(j + 1) * LANES]
            vj = v_ref[pl.ds(koff, n_win), j * LANES:(j + 1) * LANES]
            zero = jnp.zeros_like(qj)
            halves = []
            for a in range(2):
                qm = jnp.where(left, qj, zero) if a == 0 else jnp.where(left, zero, qj)
                s = _dot_nt(qm, kj) + bias_ref[2 * j + a, rho]
                p = jnp.exp(s - jnp.max(s, axis=-1, keepdims=True))
                inv = 1.0 / jnp.sum(p, axis=-1, keepdims=True)
                halves.append(_dot(p.astype(BF16), vj) * inv)
            outs.append(jnp.where(left, halves[0], halves[1]))
        o_ref[pl.ds(qoff, GRID_W), :] = jnp.concatenate(outs, axis=-1).astype(o_ref.dtype)
        return carry

    lax.fori_loop(0, rows_per_step, row_body, 0)


def _neighborhood_attention(qkv, bias, batch, seq):
    t = qkv.shape[0]
    rows = seq // GRID_W
    rps = NA_ROWS_PER_STEP
    tq = rps * GRID_W
    nqb = rows // rps
    return pl.pallas_call(
        functools.partial(_na_kernel, rows_per_step=rps),
        grid=(batch, nqb),
        in_specs=[pl.BlockSpec((tq, A_WIDTH), lambda b, i: (b * nqb + i, 0)),
                  pl.BlockSpec((seq, A_WIDTH), lambda b, i: (b, 1)),
                  pl.BlockSpec((seq, A_WIDTH), lambda b, i: (b, 2)),
                  _full_spec(bias.shape)],
        out_specs=pl.BlockSpec((tq, A_WIDTH), lambda b, i: (b * nqb + i, 0)),
        out_shape=jax.ShapeDtypeStruct((t, A_WIDTH), BF16),
        compiler_params=_params("parallel", "arbitrary"), name="nbr_attn",
    )(qkv, qkv, qkv, bias)


def _na_bias_tiles(rpb, rows):
    c = np.arange(GRID_W)
    c0 = np.clip(c - NA_COLS // 2, 0, GRID_W - NA_COLS)
    kc = np.arange(GRID_W)
    in_win = (kc[None, :] >= c0[:, None]) & (kc[None, :] < c0[:, None] + NA_COLS)
    dc = kc[None, :] - c[:, None] + (NA_COLS - 1)
    pick = (dc[None] == np.arange(2 * NA_COLS - 1)[:, None, None]) & in_win[None]
    toep = jnp.einsum("hrd,dck->hrck", rpb, jnp.asarray(pick, F32), precision=lax.Precision.HIGHEST)
    toep = jnp.where(in_win[None, None], toep, NEG_BIG)
    kr = min(NA_ROWS, rows)
    per_rho = [toep[:, NA_ROWS - 1 - rho:NA_ROWS - 1 - rho + kr] for rho in range(kr)]
    tiles = jnp.stack(per_rho, axis=1)
    tiles = tiles.transpose(0, 1, 3, 2, 4).reshape(rpb.shape[0], kr, GRID_W, kr * GRID_W)
    return tiles.astype(F32)


def _merge_kernel(ya_ref, yb_ref, yc_ref, gate_ref, h_ref, wa_ref, wb_ref, wc_ref, wo_ref,
                  g_ref, b_ref, o_ref):
    d = D_MODEL
    merged = (gate_ref[:, :d].astype(F32) * _dot(ya_ref[...], wa_ref[...])
              + gate_ref[:, d:2 * d].astype(F32) * _dot(yb_ref[...], wb_ref[...])
              + gate_ref[:, 2 * d:].astype(F32) * _dot(yc_ref[...], wc_ref[...]))
    mix = _dot(merged.astype(BF16), wo_ref[...])
    o_ref[...] = _ln_rows(ALPHA * h_ref[...] + mix, g_ref[...], b_ref[...])


def _merge_out(ya, yb, yc, gates, h, wa, wb, wc, wo, g, b):
    t, d = h.shape
    tm = ROW_TILE
    return pl.pallas_call(
        _merge_kernel, grid=(t // tm,),
        in_specs=[_row_spec(tm, ya.shape[1]), _row_spec(tm, yb.shape[1]), _row_spec(tm, yc.shape[1]),
                  _row_spec(tm, 3 * d), _row_spec(tm, d),
                  _full_spec(wa.shape), _full_spec(wb.shape), _full_spec(wc.shape), _full_spec(wo.shape),
                  _full_spec((1, d)), _full_spec((1, d))],
        out_specs=_row_spec(tm, d),
        out_shape=jax.ShapeDtypeStruct((t, d), F32),
        compiler_params=_params("parallel"), name="merge_out",
    )(ya, yb, yc, gates, h, wa, wb, wc, wo, g, b)


def _cross_kernel(h_ref, k_ref, v_ref, wq_ref, wo_ref, g_ref, b_ref, o_ref):
    h = h_ref[...]
    q = (_dot(h.astype(BF16), wq_ref[...]) * (MEM_HEAD_DIM ** -0.5)).astype(BF16)
    outs = []
    for hd in range(MEM_HEADS):
        lo, hi = hd * MEM_HEAD_DIM, (hd + 1) * MEM_HEAD_DIM
        s = _dot_nt(q[:, lo:hi], k_ref[:, lo:hi])
        p = jnp.exp(s - jnp.max(s, axis=-1, keepdims=True))
        inv = 1.0 / jnp.sum(p, axis=-1, keepdims=True)
        outs.append((_dot(p.astype(BF16), v_ref[:, lo:hi]) * inv).astype(BF16))
    xa = _dot(jnp.concatenate(outs, axis=-1), wo_ref[...])
    o_ref[...] = _ln_rows(ALPHA * h + xa, g_ref[...], b_ref[...])


def _cross_attention(h, kv, wq, wo, g, b, batch, seq):
    t, d = h.shape
    tm = ROW_TILE
    nsb = seq // tm
    m = kv.shape[0] // batch
    return pl.pallas_call(
        _cross_kernel, grid=(batch, nsb),
        in_specs=[pl.BlockSpec((tm, d), lambda bi, i: (bi * nsb + i, 0)),
                  pl.BlockSpec((m, d), lambda bi, i: (bi, 0)),
                  pl.BlockSpec((m, d), lambda bi, i: (bi, 1)),
                  _full_spec(wq.shape), _full_spec(wo.shape), _full_spec((1, d)), _full_spec((1, d))],
        out_specs=pl.BlockSpec((tm, d), lambda bi, i: (bi * nsb + i, 0)),
        out_shape=jax.ShapeDtypeStruct((t, d), F32),
        compiler_params=_params("parallel", "arbitrary"), name="cross_attn",
    )(h, kv, kv, wq, wo, g, b)


def _swiglu_acc(x, wg_ref, wu_ref, wd_ref, acc_ref):
    acc_ref[...] = jnp.zeros(acc_ref.shape, F32)

    def body(c, carry):
        a = _dot(x, wg_ref[0, c])
        u = _dot(x, wu_ref[0, c])
        act = (a * jax.nn.sigmoid(a) * u).astype(BF16)
        acc_ref[...] += _dot(act, wd_ref[0, c])
        return carry

    lax.fori_loop(0, wg_ref.shape[1], body, 0)


def _ffn_dense_kernel(h_ref, wg_ref, wu_ref, wd_ref, g_ref, b_ref, o_ref, acc_ref):
    h = h_ref[...]
    _swiglu_acc(h.astype(BF16), wg_ref, wu_ref, wd_ref, acc_ref)
    o_ref[...] = _ln_rows(ALPHA * h + acc_ref[...], g_ref[...], b_ref[...])


def _ffn_dense(h, wg, wu, wd, g, b):
    t, d = h.shape
    tm = ROW_TILE
    wspec = lambda w: pl.BlockSpec((1,) + w.shape[1:], lambda i: (0, 0, 0, 0))
    return pl.pallas_call(
        _ffn_dense_kernel, grid=(t // tm,),
        in_specs=[_row_spec(tm, d), wspec(wg), wspec(wu), wspec(wd), _full_spec((1, d)), _full_spec((1, d))],
        out_specs=_row_spec(tm, d),
        out_shape=jax.ShapeDtypeStruct((t, d), F32),
        scratch_shapes=[pltpu.VMEM((tm, d), F32)],
        compiler_params=_params("parallel"), name="ffn_dense",
    )(h, wg, wu, wd, g, b)


def _ffn_grouped_kernel(eid_ref, nvalid_ref, x_ref, wg_ref, wu_ref, wd_ref, o_ref, acc_ref):
    i = pl.program_id(0)

    @pl.when(i < nvalid_ref[0])
    def _():
        _swiglu_acc(x_ref[...].astype(BF16), wg_ref, wu_ref, wd_ref, acc_ref)
        o_ref[...] = acc_ref[...]

    @pl.when(i >= nvalid_ref[0])
    def _():
        o_ref[...] = jnp.zeros(o_ref.shape, F32)


def _ffn_grouped(xs, tile_eid, n_valid, wg, wu, wd):
    p, d = xs.shape
    tm = ROW_TILE
    wspec = lambda w: pl.BlockSpec((1,) + w.shape[1:], lambda i, eid, nv: (eid[i], 0, 0, 0))
    grid_spec = pltpu.PrefetchScalarGridSpec(
        num_scalar_prefetch=2, grid=(p // tm,),
        in_specs=[pl.BlockSpec((tm, d), lambda i, eid, nv: (i, 0)), wspec(wg), wspec(wu), wspec(wd)],
        out_specs=pl.BlockSpec((tm, d), lambda i, eid, nv: (i, 0)),
        scratch_shapes=[pltpu.VMEM((tm, d), F32)])
    return pl.pallas_call(
        _ffn_grouped_kernel, grid_spec=grid_spec,
        out_shape=jax.ShapeDtypeStruct((p, d), F32),
        compiler_params=_params("arbitrary"), name="ffn_grouped",
    )(tile_eid, n_valid, xs, wg, wu, wd)


def _router_kernel(h_ref, wh_ref, wl_ref, tri_ref, info_ref, cnt_ref, carry_ref):
    i = pl.program_id(0)

    @pl.when(i == 0)
    def _():
        carry_ref[...] = jnp.zeros(carry_ref.shape, F32)

    x = h_ref[...]
    xh = x.astype(BF16)
    xl = (x - xh.astype(F32)).astype(BF16)
    logits = _dot(xh, wh_ref[...]) + (_dot(xh, wl_ref[...]) + _dot(xl, wh_ref[...]))
    tm = x.shape[0]
    lane = lax.broadcasted_iota(jnp.int32, (tm, LANES), 1).astype(F32)
    lg = jnp.where(lane < N_EXPERTS, logits, NEG_BIG)
    m1 = jnp.max(lg, axis=-1, keepdims=True)
    i1 = jnp.min(jnp.where(lg == m1, lane, float(LANES)), axis=-1, keepdims=True)
    lg2 = jnp.where(lane == i1, NEG_BIG, lg)
    m2 = jnp.max(lg2, axis=-1, keepdims=True)
    i2 = jnp.min(jnp.where(lg2 == m2, lane, float(LANES)), axis=-1, keepdims=True)
    e = jnp.exp(m2 - m1)
    w1 = 1.0 / (1.0 + e)
    w2 = e * w1
    hot = jnp.logical_or(lane == i1, lane == i2).astype(F32)
    before = _dot(tri_ref[...], hot.astype(BF16)) + carry_ref[...]
    r1 = jnp.sum(jnp.where(lane == i1, before, 0.0), axis=-1, keepdims=True)
    r2 = jnp.sum(jnp.where(lane == i2, before, 0.0), axis=-1, keepdims=True)
    carry_ref[...] += jnp.sum(hot, axis=0, keepdims=True)
    cols = (i1, i2, w1, w2, r1, r2)
    info = jnp.zeros((tm, LANES), F32)
    for n, col in enumerate(cols):
        info = jnp.where(lane == n, col, info)
    info_ref[...] = info
    cnt_ref[...] = carry_ref[...]


def _router(h, w_router):
    t, d = h.shape
    tm = ROW_TILE
    wpad = jnp.zeros((d, LANES), F32).at[:, :N_EXPERTS].set(w_router)
    wh = wpad.astype(BF16)
    wl = (wpad - wh.astype(F32)).astype(BF16)
    tri = (np.arange(tm)[:, None] > np.arange(tm)[None, :]).astype(np.float32)
    return pl.pallas_call(
        _router_kernel, grid=(t // tm,),
        in_specs=[_row_spec(tm, d), _full_spec((d, LANES)), _full_spec((d, LANES)), _full_spec((tm, tm))],
        out_specs=[_row_spec(tm, LANES), _full_spec((1, LANES))],
        out_shape=[jax.ShapeDtypeStruct((t, LANES), F32), jax.ShapeDtypeStruct((1, LANES), F32)],
        scratch_shapes=[pltpu.VMEM((1, LANES), F32)],
        compiler_params=_params("arbitrary"), name="router",
    )(h, wh, wl, jnp.asarray(tri, BF16))


def _dispatch_kernel(pos1_ref, pos2_ref, x_ref, init_ref, xs_ref, sem):
    del init_ref
    tm = x_ref.shape[0]
    base = pl.program_id(0) * tm

    def copy(t, pos_ref):
        return pltpu.make_async_copy(x_ref.at[pl.ds(t, 1)], xs_ref.at[pl.ds(pos_ref[base + t], 1)], sem)

    def start(t, carry):
        copy(t, pos1_ref).start()
        copy(t, pos2_ref).start()
        return carry

    def wait(t, carry):
        copy(t, pos1_ref).wait()
        copy(t, pos2_ref).wait()
        return carry

    lax.fori_loop(0, tm, start, 0)
    lax.fori_loop(0, tm, wait, 0)


def _dispatch(h, pos1, pos2, n_slots):
    t, d = h.shape
    tm = ROW_TILE
    grid_spec = pltpu.PrefetchScalarGridSpec(
        num_scalar_prefetch=2, grid=(t // tm,),
        in_specs=[pl.BlockSpec((tm, d), lambda i, p1, p2: (i, 0)), pl.BlockSpec(memory_space=pl.ANY)],
        out_specs=pl.BlockSpec(memory_space=pl.ANY),
        scratch_shapes=[pltpu.SemaphoreType.DMA(())])
    return pl.pallas_call(
        _dispatch_kernel, grid_spec=grid_spec,
        out_shape=jax.ShapeDtypeStruct((n_slots, d), F32),
        input_output_aliases={3: 0},
        compiler_params=_params("arbitrary"), name="moe_dispatch",
    )(pos1, pos2, h, jnp.zeros((n_slots, d), F32))


def _combine_kernel(pos1_ref, pos2_ref, h_ref, w1_ref, w2_ref, ys_ref, g_ref, b_ref, o_ref, buf_ref, sem):
    tm = h_ref.shape[0]
    base = pl.program_id(0) * tm

    def copy(t, k, pos_ref):
        return pltpu.make_async_copy(ys_ref.at[pl.ds(pos_ref[base + t], 1)], buf_ref.at[k, pl.ds(t, 1)], sem)

    def start(t, carry):
        copy(t, 0, pos1_ref).start()
        copy(t, 1, pos2_ref).start()
        return carry

    def wait(t, carry):
        copy(t, 0, pos1_ref).wait()
        copy(t, 1, pos2_ref).wait()
        return carry

    lax.fori_loop(0, tm, start, 0)
    lax.fori_loop(0, tm, wait, 0)
    f = w1_ref[...] * buf_ref[0] + w2_ref[...] * buf_ref[1]
    o_ref[...] = _ln_rows(ALPHA * h_ref[...] + f, g_ref[...], b_ref[...])


def _combine(h, ys, pos1, pos2, w1, w2, g, b):
    t, d = h.shape
    tm = ROW_TILE
    row = lambda n: pl.BlockSpec((tm, n), lambda i, p1, p2: (i, 0))
    full = lambda s: pl.BlockSpec(s, lambda i, p1, p2: (0, 0))
    grid_spec = pltpu.PrefetchScalarGridSpec(
        num_scalar_prefetch=2, grid=(t // tm,),
        in_specs=[row(d), row(1), row(1), pl.BlockSpec(memory_space=pl.ANY), full((1, d)), full((1, d))],
        out_specs=row(d),
        scratch_shapes=[pltpu.VMEM((2, tm, d), F32), pltpu.SemaphoreType.DMA(())])
    return pl.pallas_call(
        _combine_kernel, grid_spec=grid_spec,
        out_shape=jax.ShapeDtypeStruct((t, d), F32),
        compiler_params=_params("arbitrary"), name="moe_combine",
    )(pos1, pos2, h, w1, w2, ys, g, b)


def _moe_layer(h, w_router, wg, wu, wd, g, b):
    t, d = h.shape
    tm = ROW_TILE
    info, cnt = _router(h, w_router)
    idx1 = info[:, 0].astype(jnp.int32)
    idx2 = info[:, 1].astype(jnp.int32)
    counts = cnt[0, :N_EXPERTS].astype(jnp.int32)
    padded = ((counts + tm - 1) // tm) * tm
    ends = jnp.cumsum(padded)
    offs = ends - padded
    experts = jnp.arange(N_EXPERTS, dtype=jnp.int32)
    group_start = lambda idx: jnp.sum(jnp.where(idx[:, None] == experts[None, :], offs[None, :], 0), axis=1)
    pos1 = group_start(idx1) + info[:, 4].astype(jnp.int32)
    pos2 = group_start(idx2) + info[:, 5].astype(jnp.int32)
    n_tiles = (2 * t) // tm + N_EXPERTS
    tile_start = jnp.arange(n_tiles, dtype=jnp.int32) * tm
    tile_eid = jnp.minimum(jnp.sum((ends[None, :] <= tile_start[:, None]).astype(jnp.int32), axis=1),
                           N_EXPERTS - 1)
    n_valid = (ends[-1:] // tm).astype(jnp.int32)
    xs = _dispatch(h, pos1, pos2, n_tiles * tm)
    ys = _ffn_grouped(xs, tile_eid, n_valid, wg, wu, wd)
    return _combine(h, ys, pos1, pos2, info[:, 2:3], info[:, 3:4], g, b)


def _rope_tables(seq):
    tpos = jnp.arange(seq)
    row = (tpos // GRID_W).astype(F32)
    col = (tpos % GRID_W).astype(F32)

    def angles(rot_dim):
        n = rot_dim // 4
        inv = ROPE_THETA ** (-jnp.arange(n, dtype=F32) / n)
        ang = jnp.concatenate([row[:, None] * inv, col[:, None] * inv], axis=-1)
        return jnp.cos(ang), jnp.sin(ang)

    cb, sb = angles(HEAD_DIM)
    cos_b = jnp.concatenate([cb, cb, cb, cb], axis=-1)
    sin_b = jnp.concatenate([-sb, sb, -sb, sb], axis=-1)
    cc, sc = angles(MLA_ROPE)
    ones = jnp.ones((seq, MLA_NOPE), F32)
    z64 = jnp.zeros((seq, MLA_NOPE), F32)
    z32 = jnp.zeros((seq, LANES - MLA_NOPE - MLA_ROPE), F32)
    cos_c = jnp.concatenate([ones, cc, cc, z32], axis=-1)
    sin_c = jnp.concatenate([z64, -sc, sc, z32], axis=-1)
    return cos_b, sin_b, cos_c, sin_c


def _chunk_cols(w):
    lead = w.shape[:-2]
    k = w.shape[-2]
    w = w.reshape(lead + (k, N_FF_CHUNKS, FF_CHUNK))
    return jnp.swapaxes(w, -3, -2).astype(BF16)


def _ffn_weights(w_gu, w_down):
    wg = _chunk_cols(w_gu[..., :D_FF])
    wu = _chunk_cols(w_gu[..., D_FF:])
    wd = w_down.reshape(w_down.shape[0], N_FF_CHUNKS, FF_CHUNK, w_down.shape[-1]).astype(BF16)
    return wg, wu, wd


def _mixer_weights(w_in, gq, gk, w_uq, w_ukv, w_proj_b):
    o = IN_OFFS
    d = w_in.shape[0]
    half = HEAD_DIM // 2
    w_a = jnp.concatenate([w_in[:, o[0]:o[1]] * (HEAD_DIM ** -0.5), w_in[:, o[1]:o[3]]], axis=1)
    pair_order = np.array([0, 4, 1, 5, 2, 6, 3, 7])
    wq = w_in[:, o[3]:o[4]].reshape(d, GQA_HEADS, HEAD_DIM)[:, pair_order]
    wk = w_in[:, o[4]:o[5]].reshape(d, GQA_KV_HEADS, HEAD_DIM)
    w_b = jnp.concatenate([wq.reshape(d, -1), jnp.roll(wq, half, axis=-1).reshape(d, -1),
                           wk.reshape(d, -1), jnp.roll(wk, half, axis=-1).reshape(d, -1),
                           w_in[:, o[5]:o[6]]], axis=1)
    g2 = lambda g: jnp.tile(g, LANES // HEAD_DIM).reshape(1, LANES)
    gains_b = (g2(gq), g2(jnp.roll(gq, half)), g2(gk), g2(jnp.roll(gk, half)))
    wkr = w_in[:, o[8]:o[9]]
    pad_kr = lambda w: jnp.concatenate(
        [jnp.zeros((d, MLA_NOPE), F32), w, jnp.zeros((d, LANES - MLA_NOPE - MLA_ROPE), F32)], axis=1)
    w_c = jnp.concatenate([w_in[:, o[6]:o[8]], pad_kr(wkr), pad_kr(jnp.roll(wkr, MLA_ROPE // 2, axis=-1))],
                          axis=1)
    uq = w_uq.reshape(MLA_Q_RANK, MLA_HEADS, MLA_NOPE + MLA_ROPE)
    zq = lambda n: jnp.zeros((MLA_Q_RANK, MLA_HEADS, n), F32)
    q_pad = jnp.concatenate([uq, zq(LANES - MLA_NOPE - MLA_ROPE)], axis=-1)
    q_rot = jnp.concatenate([zq(MLA_NOPE), jnp.roll(uq[..., MLA_NOPE:], MLA_ROPE // 2, axis=-1),
                             zq(LANES - MLA_NOPE - MLA_ROPE)], axis=-1)
    wuq = jnp.concatenate([q_pad.reshape(MLA_Q_RANK, -1), q_rot.reshape(MLA_Q_RANK, -1)], axis=1)
    ukv = w_ukv.reshape(MLA_KV_RANK, MLA_HEADS, MLA_NOPE + MLA_V)
    k_part = jnp.concatenate([ukv[..., :MLA_NOPE], jnp.zeros((MLA_KV_RANK, MLA_HEADS, LANES - MLA_NOPE), F32)],
                             axis=-1)
    wukv = jnp.concatenate([k_part.reshape(MLA_KV_RANK, -1), ukv[..., MLA_NOPE:].reshape(MLA_KV_RANK, -1)],
                           axis=1)
    wpb = w_proj_b.reshape(GQA_HEADS, HEAD_DIM, -1)[pair_order].reshape(B_WIDTH, -1)
    bf = lambda w: w.astype(BF16)
    return bf(w_a), bf(w_b), gains_b, bf(w_c), bf(wuq), bf(wukv), bf(w_in[:, o[9]:]), bf(wpb)


def kernel(x, mem, emb_ln_g, emb_ln_b, w_in, na_rpb, gqa_q_norm, gqa_k_norm, mla_q_norm, mla_kv_norm,
           mla_w_uq, mla_w_ukv, w_proj_a, w_proj_b, w_proj_c, w_out, mem_wq, mem_wkv, mem_wo, ln_g, ln_b,
           ffn_w_gu, ffn_w_down, moe_router, moe_w_gu, moe_w_down):
    batch, seq, d = x.shape
    t = batch * seq
    cos_b, sin_b, cos_c, sin_c = _rope_tables(seq)
    seg = np.kron(np.eye(LANES // HEAD_DIM), np.full((HEAD_DIM, HEAD_DIM), 1.0 / HEAD_DIM))
    seg = jnp.asarray(seg, BF16)
    mem2 = mem.reshape(-1, d)
    ffn_w = _ffn_weights(ffn_w_gu, ffn_w_down)
    vec = lambda v: v.reshape(1, -1)

    h = _layer_norm(x.reshape(t, d), emb_ln_g, emb_ln_b)
    for l in range(DEPTH):
        w_a, w_b, gains_b, w_c, wuq, wukv, w_g, wpb = _mixer_weights(
            w_in[l], gqa_q_norm[l], gqa_k_norm[l], mla_w_uq[l], mla_w_ukv[l], w_proj_b[l])
        qkv_a = _proj(h, w_a, name="proj_a")
        ya = _neighborhood_attention(qkv_a, _na_bias_tiles(na_rpb[l], seq // GRID_W), batch, seq)
        qb, kb, vb = _b_prologue(h, w_b, *gains_b, cos_b, sin_b, seg, seq)
        yb = _paired_attention(qb, kb, vb, batch, seq, shared=True)
        qc, kc, vc = _c_prologue(h, w_c, vec(mla_q_norm[l]), vec(mla_kv_norm[l]), wuq, wukv, cos_c, sin_c, seq)
        yc = _paired_attention(qc, kc, vc, batch, seq, shared=False)
        gates = _proj(h, w_g, act="sigmoid", name="proj_gates")
        h = _merge_out(ya, yb, yc, gates, h, w_proj_a[l].astype(BF16), wpb, w_proj_c[l].astype(BF16),
                       w_out[l].astype(BF16), vec(ln_g[l, 0]), vec(ln_b[l, 0]))
        kv = _proj(mem2, mem_wkv[l].astype(BF16), tm=256, name="proj_mem_kv")
        h = _cross_attention(h, kv, mem_wq[l].astype(BF16), mem_wo[l].astype(BF16),
                             vec(ln_g[l, 1]), vec(ln_b[l, 1]), batch, seq)
        if l % 2 == 0:
            wg, wu, wd = (w[l // 2:l // 2 + 1] for w in ffn_w)
            h = _ffn_dense(h, wg, wu, wd, vec(ln_g[l, 2]), vec(ln_b[l, 2]))
        else:
            wg, wu, wd = _ffn_weights(moe_w_gu[l // 2], moe_w_down[l // 2])
            h = _moe_layer(h, moe_router[l // 2], wg, wu, wd, vec(ln_g[l, 2]), vec(ln_b[l, 2]))
    return h.reshape(batch, seq, d)
```

```python
import functools

import jax
import jax.numpy as jnp
import numpy as np
from jax import lax
from jax.experimental import pallas as pl
from jax.experimental.pallas import tpu as pltpu

F32 = jnp.float32
BF16 = jnp.bfloat16

D_MODEL = 1024
DEPTH = 4
GRID_W = 64
HEAD_DIM = 64
ROPE_THETA = 10000.0
LN_EPS = 1e-5
RMS_EPS = 1e-6
NA_HEADS = 4
NA_ROWS = 8
NA_COLS = 16
GQA_HEADS = 8
GQA_KV_HEADS = 2
MLA_HEADS = 4
MLA_Q_RANK = 256
MLA_KV_RANK = 128
MLA_NOPE = 64
MLA_ROPE = 32
MLA_V = 64
A_WIDTH = NA_HEADS * HEAD_DIM
B_WIDTH = GQA_HEADS * HEAD_DIM
C_WIDTH = MLA_HEADS * MLA_V
IN_SPLITS = (A_WIDTH, A_WIDTH, A_WIDTH,
             B_WIDTH, GQA_KV_HEADS * HEAD_DIM, GQA_KV_HEADS * HEAD_DIM,
             MLA_Q_RANK, MLA_KV_RANK, MLA_ROPE,
             3 * D_MODEL)
IN_OFFS = tuple(int(v) for v in np.cumsum((0,) + IN_SPLITS))
MEM_HEADS = 4
MEM_HEAD_DIM = D_MODEL // MEM_HEADS
D_FF = ((8 * D_MODEL // 3 + 127) // 128) * 128
N_EXPERTS = 8
ALPHA = (2 * DEPTH) ** 0.25

LANES = 128
FF_CHUNK = 256
ROW_TILE = 512
ATT_TQ = 512
NA_ROWS_PER_STEP = 8
NEG_BIG = -1e30
LOG2E = 1.4426950408889634
ROW_L0 = HEAD_DIM
ROW_L1 = 0
VMEM_LIMIT = 56 * 1024 * 1024


def _params(*sem):
    return pltpu.CompilerParams(dimension_semantics=sem, vmem_limit_bytes=VMEM_LIMIT)


def _dot(a, b):
    return jnp.dot(a, b, preferred_element_type=F32)


def _dot_nt(a, b):
    return lax.dot_general(a, b, (((1,), (1,)), ((), ())), preferred_element_type=F32)


def _ln_rows(y, g, b):
    mu = jnp.mean(y, axis=-1, keepdims=True)
    d = y - mu
    var = jnp.mean(d * d, axis=-1, keepdims=True)
    return d * lax.rsqrt(var + LN_EPS) * g + b


def _row_spec(tm, n):
    return pl.BlockSpec((tm, n), lambda i: (i, 0))


def _full_spec(shape):
    nd = len(shape)
    return pl.BlockSpec(shape, lambda *_: (0,) * nd)


def _ln_kernel(x_ref, g_ref, b_ref, o_ref):
    o_ref[...] = _ln_rows(x_ref[...], g_ref[...], b_ref[...])


def _layer_norm(x, g, b):
    t, d = x.shape
    return pl.pallas_call(
        _ln_kernel, grid=(t // ROW_TILE,),
        in_specs=[_row_spec(ROW_TILE, d), _full_spec((1, d)), _full_spec((1, d))],
        out_specs=_row_spec(ROW_TILE, d),
        out_shape=jax.ShapeDtypeStruct((t, d), F32),
        compiler_params=_params("parallel"), name="embed_ln",
    )(x, g.reshape(1, d), b.reshape(1, d))


def _proj_kernel(x_ref, w_ref, o_ref, *, act, n_chunk):
    x = x_ref[...].astype(BF16)
    n = w_ref.shape[1]
    for c in range(0, n, n_chunk):
        e = min(c + n_chunk, n)
        y = _dot(x, w_ref[:, c:e])
        if act == "sigmoid":
            y = jax.nn.sigmoid(y)
        o_ref[:, c:e] = y.astype(o_ref.dtype)


def _proj(x, w, act=None, tm=ROW_TILE, n_chunk=512, name="proj"):
    t, k = x.shape
    n = w.shape[1]
    n_chunk = min(n_chunk, n)
    return pl.pallas_call(
        functools.partial(_proj_kernel, act=act, n_chunk=n_chunk), grid=(t // tm,),
        in_specs=[_row_spec(tm, k), _full_spec((k, n))],
        out_specs=_row_spec(tm, n),
        out_shape=jax.ShapeDtypeStruct((t, n), BF16),
        compiler_params=_params("parallel"), name=name,
    )(x, w)


def _bpro_kernel(x_ref, w_ref, gq_ref, gqr_ref, gk_ref, gkr_ref, cos_ref, sin_ref, seg_ref,
                 q_ref, k_ref, v_ref):
    x = x_ref[...].astype(BF16)
    cos = cos_ref[...]
    sin = sin_ref[...]
    seg = seg_ref[...]

    def norm_rope(z, zr, g, gr):
        ms = _dot((z * z).astype(BF16), seg)
        return (z * g * cos + zr * gr * sin) * lax.rsqrt(ms + RMS_EPS)

    nq = B_WIDTH
    z = _dot(x, w_ref[:, :2 * nq])
    for j in range(nq // LANES):
        lo, hi = j * LANES, (j + 1) * LANES
        q = norm_rope(z[:, lo:hi], z[:, nq + lo:nq + hi], gq_ref[...], gqr_ref[...])
        q_ref[:, lo:hi] = (q * (HEAD_DIM ** -0.5 * LOG2E)).astype(BF16)
    zk = _dot(x, w_ref[:, 2 * nq:])
    k_ref[...] = norm_rope(zk[:, :LANES], zk[:, LANES:2 * LANES], gk_ref[...], gkr_ref[...]).astype(BF16)
    v_ref[0] = _value_rows(zk[:, 2 * LANES:])


def _b_prologue(h, w, gq, gqr, gk, gkr, cos, sin, seg, seq):
    t, d = h.shape
    tm = ROW_TILE
    nsb = seq // tm
    tab = pl.BlockSpec((tm, LANES), lambda i: (i % nsb, 0))
    vec = _full_spec((1, LANES))
    return pl.pallas_call(
        _bpro_kernel, grid=(t // tm,),
        in_specs=[_row_spec(tm, d), _full_spec(w.shape), vec, vec, vec, vec, tab, tab,
                  _full_spec((LANES, LANES))],
        out_specs=[_row_spec(tm, B_WIDTH), _row_spec(tm, LANES),
                   pl.BlockSpec((1, 2 * LANES, tm), lambda i: (i, 0, 0))],
        out_shape=[jax.ShapeDtypeStruct((t, B_WIDTH), BF16),
                   jax.ShapeDtypeStruct((t, LANES), BF16),
                   jax.ShapeDtypeStruct((t // tm, 2 * LANES, tm), BF16)],
        compiler_params=_params("parallel"), name="b_prologue",
    )(h, w, gq, gqr, gk, gkr, cos, sin, seg)


def _cpro_kernel(x_ref, w_ref, gq_ref, gkv_ref, wuq_ref, wukv_ref, cos_ref, sin_ref,
                 q_ref, k_ref, v_ref):
    x = x_ref[...].astype(BF16)
    cos = cos_ref[...]
    sin = sin_ref[...]
    z = _dot(x, w_ref[...])
    nh = MLA_HEADS
    scale = (MLA_NOPE + MLA_ROPE) ** -0.5 * LOG2E

    def rms(v, g):
        return (v * lax.rsqrt(jnp.mean(v * v, axis=-1, keepdims=True) + RMS_EPS) * g).astype(BF16)

    qq = _dot(rms(z[:, :MLA_Q_RANK], gq_ref[...]), wuq_ref[...])
    for hd in range(nh):
        lo, hi = hd * LANES, (hd + 1) * LANES
        q = qq[:, lo:hi] * cos + qq[:, nh * LANES + lo:nh * LANES + hi] * sin
        q_ref[:, lo:hi] = (q * scale).astype(BF16)
    o = MLA_Q_RANK + MLA_KV_RANK
    kk = _dot(rms(z[:, MLA_Q_RANK:o], gkv_ref[...]), wukv_ref[...])
    kpe = z[:, o:o + LANES] * cos + z[:, o + LANES:o + 2 * LANES] * sin
    for hd in range(nh):
        lo, hi = hd * LANES, (hd + 1) * LANES
        k_ref[:, lo:hi] = (kk[:, lo:hi] + kpe).astype(BF16)
    for j in range(C_WIDTH // LANES):
        vj = kk[:, (nh + j) * LANES:(nh + j + 1) * LANES]
        v_ref[0, 2 * j * LANES:2 * (j + 1) * LANES, :] = _value_rows(vj)


def _c_prologue(h, w, gq, gkv, wuq, wukv, cos, sin, seq):
    t, d = h.shape
    tm = ROW_TILE
    nsb = seq // tm
    tab = pl.BlockSpec((tm, LANES), lambda i: (i % nsb, 0))
    nq = MLA_HEADS * LANES
    return pl.pallas_call(
        _cpro_kernel, grid=(t // tm,),
        in_specs=[_row_spec(tm, d), _full_spec(w.shape), _full_spec(gq.shape), _full_spec(gkv.shape),
                  _full_spec(wuq.shape), _full_spec(wukv.shape), tab, tab],
        out_specs=[_row_spec(tm, nq), _row_spec(tm, nq),
                   pl.BlockSpec((1, 2 * C_WIDTH, tm), lambda i: (i, 0, 0))],
        out_shape=[jax.ShapeDtypeStruct((t, nq), BF16),
                   jax.ShapeDtypeStruct((t, nq), BF16),
                   jax.ShapeDtypeStruct((t // tm, 2 * C_WIDTH, tm), BF16)],
        compiler_params=_params("parallel"), name="c_prologue",
    )(h, w, gq, gkv, wuq, wukv, cos, sin)


def _attn_kernel(q_ref, k_ref, vt_ref, o_ref, m_ref, mx_ref, acc_ref, st0_ref, st1_ref, *, shared):
    tq = o_ref.shape[0]
    n_chunks, _, tk = vt_ref.shape
    if shared:
        left = lax.broadcasted_iota(jnp.int32, (tq, LANES), 1) < HEAD_DIM
        q2 = q_ref[...]
        zero = jnp.zeros_like(q2)
        qs = (jnp.where(left, q2, zero), jnp.where(left, zero, q2))
    else:
        qs = (q_ref[:, :LANES], q_ref[:, LANES:])
    m_ref[...] = jnp.full(m_ref.shape, NEG_BIG, F32)
    acc_ref[...] = jnp.zeros(acc_ref.shape, F32)

    def scores(c, st_ref):
        off = pl.multiple_of(c * tk, tk)
        for a in range(2):
            if shared:
                kc = k_ref[pl.ds(off, tk), :]
            else:
                kc = k_ref[pl.ds(off, tk), a * LANES:(a + 1) * LANES]
            st = _dot_nt(kc, qs[a])
            st_ref[a] = st
            mx_ref[c % 2, a] = jnp.max(st, axis=0, keepdims=True)

    def consume(c, st_ref):
        for a in range(2):
            st = st_ref[a]
            m_prev = m_ref[a]
            m_new = jnp.maximum(m_prev, mx_ref[c % 2, a])
            alpha = jnp.exp2(m_prev - m_new)
            p = jnp.exp2(st - m_new).astype(BF16)
            acc_ref[a] = alpha * acc_ref[a] + _dot(vt_ref[c, a * LANES:(a + 1) * LANES, :], p)
            m_ref[a] = m_new

    bufs = (st0_ref, st1_ref)
    scores(0, st0_ref)
    for c in range(n_chunks):
        if c + 1 < n_chunks:
            scores(c + 1, bufs[(c + 1) % 2])
        consume(c, bufs[c % 2])
    top = lax.broadcasted_iota(jnp.int32, (LANES, tq), 0) < HEAD_DIM
    acc0 = acc_ref[0]
    acc1 = acc_ref[1]
    ot = jnp.where(top, acc0 / acc0[ROW_L0:ROW_L0 + 1], acc1 / acc1[ROW_L1:ROW_L1 + 1])
    o_ref[...] = ot.T.astype(o_ref.dtype)


def _value_rows(v):
    vt = v.T
    row = lax.broadcasted_iota(jnp.int32, vt.shape, 0)
    v0 = jnp.where(row < HEAD_DIM, vt, jnp.where(row == ROW_L0, 1.0, 0.0))
    v1 = jnp.where(row >= HEAD_DIM, vt, jnp.where(row == ROW_L1, 1.0, 0.0))
    return jnp.concatenate([v0, v1], axis=0).astype(BF16)


def _paired_attention(q, k, vt, batch, seq, shared):
    t = q.shape[0]
    tk = vt.shape[2]
    n_pairs = vt.shape[1] // (2 * LANES) if not shared else q.shape[1] // LANES
    tq = ATT_TQ
    nqb = seq // tq
    nkc = seq // tk
    qw = LANES if shared else 2 * LANES
    q_spec = pl.BlockSpec((tq, qw), lambda b, j, i: (b * nqb + i, j))
    if shared:
        k_spec = pl.BlockSpec((seq, LANES), lambda b, j, i: (b, 0))
        v_spec = pl.BlockSpec((nkc, 2 * LANES, tk), lambda b, j, i: (b, 0, 0))
    else:
        k_spec = pl.BlockSpec((seq, 2 * LANES), lambda b, j, i: (b, j))
        v_spec = pl.BlockSpec((nkc, 2 * LANES, tk), lambda b, j, i: (b, j, 0))
    return pl.pallas_call(
        functools.partial(_attn_kernel, shared=shared),
        grid=(batch, n_pairs, nqb),
        in_specs=[q_spec, k_spec, v_spec],
        out_specs=pl.BlockSpec((tq, LANES), lambda b, j, i: (b * nqb + i, j)),
        out_shape=jax.ShapeDtypeStruct((t, n_pairs * LANES), BF16),
        scratch_shapes=[pltpu.VMEM((2, 1, tq), F32), pltpu.VMEM((2, 2, 1, tq), F32),
                        pltpu.VMEM((2, LANES, tq), F32),
                        pltpu.VMEM((2, tk, tq), F32), pltpu.VMEM((2, tk, tq), F32)],
        compiler_params=_params("parallel", "parallel", "arbitrary"),
        name="attn_shared" if shared else "attn_split",
    )(q, k, vt)


def _na_kernel(q_ref, k_ref, v_ref, bias_ref, o_ref, *, rows_per_step):
    seq = k_ref.shape[0]
    rows = seq // GRID_W
    n_win = NA_ROWS * GRID_W
    left = lax.broadcasted_iota(jnp.int32, (GRID_W, LANES), 1) < HEAD_DIM

    def row_body(i, carry):
        r = pl.program_id(1) * rows_per_step + i
        r0 = jnp.clip(r - NA_ROWS // 2, 0, rows - NA_ROWS)
        rho = r - r0
        qoff = pl.multiple_of(i * GRID_W, GRID_W)
        koff = pl.multiple_of(r0 * GRID_W, GRID_W)
        heads = [(j, a) for j in range(A_WIDTH // LANES) for a in range(2)]
        scores = []
        for j, a in heads:
            qj = q_ref[pl.ds(qoff, GRID_W), j * LANES:(j + 1) * LANES]
            kj = k_ref[pl.ds(koff, n_win), j * LANES:(j + 1) * LANES]
            zero = jnp.zeros_like(qj)
            qm = jnp.where(left, qj, zero) if a == 0 else jnp.where(left, zero, qj)
            scores.append(_dot_nt(qm, kj) + bias_ref[2 * j + a, rho])
        probs = [jnp.exp(s - jnp.max(s, axis=-1, keepdims=True)) for s in scores]
        invs = [1.0 / jnp.sum(p, axis=-1, keepdims=True) for p in probs]
        outs = []
        for (j, a), p, inv in zip(heads, probs, invs):
            vj = v_ref[pl.ds(koff, n_win), j * LANES:(j + 1) * LANES]
            outs.append(_dot(p.astype(BF16), vj) * inv)
        merged = [jnp.where(left, outs[2 * j], outs[2 * j + 1]) for j in range(A_WIDTH // LANES)]
        o_ref[pl.ds(qoff, GRID_W), :] = jnp.concatenate(merged, axis=-1).astype(o_ref.dtype)
        return carry

    lax.fori_loop(0, rows_per_step, row_body, 0, unroll=2)


def _neighborhood_attention(qkv, bias, batch, seq):
    t = qkv.shape[0]
    rows = seq // GRID_W
    rps = NA_ROWS_PER_STEP
    tq = rps * GRID_W
    nqb = rows // rps
    return pl.pallas_call(
        functools.partial(_na_kernel, rows_per_step=rps),
        grid=(batch, nqb),
        in_specs=[pl.BlockSpec((tq, A_WIDTH), lambda b, i: (b * nqb + i, 0)),
                  pl.BlockSpec((seq, A_WIDTH), lambda b, i: (b, 1)),
                  pl.BlockSpec((seq, A_WIDTH), lambda b, i: (b, 2)),
                  _full_spec(bias.shape)],
        out_specs=pl.BlockSpec((tq, A_WIDTH), lambda b, i: (b * nqb + i, 0)),
        out_shape=jax.ShapeDtypeStruct((t, A_WIDTH), BF16),
        compiler_params=_params("parallel", "arbitrary"), name="nbr_attn",
    )(qkv, qkv, qkv, bias)


def _na_bias_tiles(rpb, rows):
    c = np.arange(GRID_W)
    c0 = np.clip(c - NA_COLS // 2, 0, GRID_W - NA_COLS)
    kc = np.arange(GRID_W)
    in_win = (kc[None, :] >= c0[:, None]) & (kc[None, :] < c0[:, None] + NA_COLS)
    dc = kc[None, :] - c[:, None] + (NA_COLS - 1)
    pick = (dc[None] == np.arange(2 * NA_COLS - 1)[:, None, None]) & in_win[None]
    lead = rpb.shape[:-2]
    toep = jnp.einsum("hrd,dck->hrck", rpb.reshape((-1,) + rpb.shape[-2:]), jnp.asarray(pick, F32),
                      precision=lax.Precision.HIGHEST)
    toep = jnp.where(in_win[None, None], toep, NEG_BIG)
    kr = min(NA_ROWS, rows)
    per_rho = [toep[:, NA_ROWS - 1 - rho:NA_ROWS - 1 - rho + kr] for rho in range(kr)]
    tiles = jnp.stack(per_rho, axis=1)
    tiles = tiles.transpose(0, 1, 3, 2, 4).reshape(lead + (kr, GRID_W, kr * GRID_W))
    return tiles.astype(F32)


def _merge_kernel(ya_ref, yb_ref, yc_ref, gate_ref, h_ref, wa_ref, wb_ref, wc_ref, wo_ref,
                  g_ref, b_ref, o_ref):
    d = D_MODEL
    merged = (gate_ref[:, :d].astype(F32) * _dot(ya_ref[...], wa_ref[...])
              + gate_ref[:, d:2 * d].astype(F32) * _dot(yb_ref[...], wb_ref[...])
              + gate_ref[:, 2 * d:].astype(F32) * _dot(yc_ref[...], wc_ref[...]))
    o_ref[...] = _dot(merged.astype(BF16), wo_ref[...])
    _residual_ln_inplace(h_ref, g_ref, b_ref, o_ref)


def _merge_out(ya, yb, yc, gates, h, wa, wb, wc, wo, g, b):
    t, d = h.shape
    tm = ROW_TILE
    return pl.pallas_call(
        _merge_kernel, grid=(t // tm,),
        in_specs=[_row_spec(tm, ya.shape[1]), _row_spec(tm, yb.shape[1]), _row_spec(tm, yc.shape[1]),
                  _row_spec(tm, 3 * d), _row_spec(tm, d),
                  _full_spec(wa.shape), _full_spec(wb.shape), _full_spec(wc.shape), _full_spec(wo.shape),
                  _full_spec((1, d)), _full_spec((1, d))],
        out_specs=_row_spec(tm, d),
        out_shape=jax.ShapeDtypeStruct((t, d), F32),
        compiler_params=_params("parallel"), name="merge_out",
    )(ya, yb, yc, gates, h, wa, wb, wc, wo, g, b)


def _cross_kernel(h_ref, k_ref, v_ref, wq_ref, wo_ref, g_ref, b_ref, o_ref):
    q = (_dot(h_ref[...].astype(BF16), wq_ref[...]) * (MEM_HEAD_DIM ** -0.5)).astype(BF16)
    outs = []
    for hd in range(MEM_HEADS):
        lo, hi = hd * MEM_HEAD_DIM, (hd + 1) * MEM_HEAD_DIM
        s = _dot_nt(q[:, lo:hi], k_ref[:, lo:hi])
        p = jnp.exp(s - jnp.max(s, axis=-1, keepdims=True))
        inv = 1.0 / jnp.sum(p, axis=-1, keepdims=True)
        outs.append((_dot(p.astype(BF16), v_ref[:, lo:hi]) * inv).astype(BF16))
    o_ref[...] = _dot(jnp.concatenate(outs, axis=-1), wo_ref[...])
    _residual_ln_inplace(h_ref, g_ref, b_ref, o_ref)


def _cross_attention(h, kv, wq, wo, g, b, batch, seq):
    t, d = h.shape
    tm = ROW_TILE
    nsb = seq // tm
    m = kv.shape[0] // batch
    return pl.pallas_call(
        _cross_kernel, grid=(batch, nsb),
        in_specs=[pl.BlockSpec((tm, d), lambda bi, i: (bi * nsb + i, 0)),
                  pl.BlockSpec((m, d), lambda bi, i: (bi, 0)),
                  pl.BlockSpec((m, d), lambda bi, i: (bi, 1)),
                  _full_spec(wq.shape), _full_spec(wo.shape), _full_spec((1, d)), _full_spec((1, d))],
        out_specs=pl.BlockSpec((tm, d), lambda bi, i: (bi * nsb + i, 0)),
        out_shape=jax.ShapeDtypeStruct((t, d), F32),
        compiler_params=_params("parallel", "arbitrary"), name="cross_attn",
    )(h, kv, kv, wq, wo, g, b)


def _swiglu(x, wgu_ref, wd_ref, act_ref):
    for c in range(0, D_FF, FF_CHUNK):
        a = _dot(x, wgu_ref[0, :, c:c + FF_CHUNK])
        u = _dot(x, wgu_ref[0, :, D_FF + c:D_FF + c + FF_CHUNK])
        act_ref[:, c:c + FF_CHUNK] = (a * jax.nn.sigmoid(a) * u).astype(BF16)
    return _dot(act_ref[...], wd_ref[0])


def _residual_ln_inplace(h_ref, g_ref, b_ref, o_ref):
    o_ref[...] = _ln_rows(ALPHA * h_ref[...] + o_ref[...], g_ref[...], b_ref[...])


def _ffn_dense_kernel(h_ref, wgu_ref, wd_ref, g_ref, b_ref, o_ref, act_ref):
    o_ref[...] = _swiglu(h_ref[...].astype(BF16), wgu_ref, wd_ref, act_ref)
    _residual_ln_inplace(h_ref, g_ref, b_ref, o_ref)


def _ffn_dense(h, wgu, wd, g, b):
    t, d = h.shape
    tm = ROW_TILE
    wspec = lambda w: pl.BlockSpec((1,) + w.shape[1:], lambda i: (0, 0, 0))
    return pl.pallas_call(
        _ffn_dense_kernel, grid=(t // tm,),
        in_specs=[_row_spec(tm, d), wspec(wgu), wspec(wd), _full_spec((1, d)), _full_spec((1, d))],
        out_specs=_row_spec(tm, d),
        out_shape=jax.ShapeDtypeStruct((t, d), F32),
        scratch_shapes=[pltpu.VMEM((tm, D_FF), BF16)],
        compiler_params=_params("parallel"), name="ffn_dense",
    )(h, wgu, wd, g, b)


def _ffn_grouped_kernel(eid_ref, nvalid_ref, x_ref, wgu_ref, wd_ref, o_ref, act_ref):
    i = pl.program_id(0)

    @pl.when(i < nvalid_ref[0])
    def _():
        o_ref[...] = _swiglu(x_ref[...].astype(BF16), wgu_ref, wd_ref, act_ref)

    @pl.when(i >= nvalid_ref[0])
    def _():
        o_ref[...] = jnp.zeros(o_ref.shape, F32)


def _ffn_grouped(xs, tile_eid, n_valid, wgu, wd):
    p, d = xs.shape
    tm = ROW_TILE
    wspec = lambda w: pl.BlockSpec((1,) + w.shape[1:], lambda i, eid, nv: (eid[i], 0, 0))
    grid_spec = pltpu.PrefetchScalarGridSpec(
        num_scalar_prefetch=2, grid=(p // tm,),
        in_specs=[pl.BlockSpec((tm, d), lambda i, eid, nv: (i, 0)), wspec(wgu), wspec(wd)],
        out_specs=pl.BlockSpec((tm, d), lambda i, eid, nv: (i, 0)),
        scratch_shapes=[pltpu.VMEM((tm, D_FF), BF16)])
    return pl.pallas_call(
        _ffn_grouped_kernel, grid_spec=grid_spec,
        out_shape=jax.ShapeDtypeStruct((p, d), F32),
        compiler_params=_params("arbitrary"), name="ffn_grouped",
    )(tile_eid, n_valid, xs, wgu, wd)


def _router_kernel(h_ref, wh_ref, wl_ref, tri_ref, info_ref, cnt_ref, carry_ref):
    i = pl.program_id(0)

    @pl.when(i == 0)
    def _():
        carry_ref[...] = jnp.zeros(carry_ref.shape, F32)

    x = h_ref[...]
    xh = x.astype(BF16)
    xl = (x - xh.astype(F32)).astype(BF16)
    logits = _dot(xh, wh_ref[...]) + (_dot(xh, wl_ref[...]) + _dot(xl, wh_ref[...]))
    tm = x.shape[0]
    lane = lax.broadcasted_iota(jnp.int32, (tm, LANES), 1).astype(F32)
    lg = jnp.where(lane < N_EXPERTS, logits, NEG_BIG)
    m1 = jnp.max(lg, axis=-1, keepdims=True)
    i1 = jnp.min(jnp.where(lg == m1, lane, float(LANES)), axis=-1, keepdims=True)
    lg2 = jnp.where(lane == i1, NEG_BIG, lg)
    m2 = jnp.max(lg2, axis=-1, keepdims=True)
    i2 = jnp.min(jnp.where(lg2 == m2, lane, float(LANES)), axis=-1, keepdims=True)
    e = jnp.exp(m2 - m1)
    w1 = 1.0 / (1.0 + e)
    w2 = e * w1
    hot = jnp.logical_or(lane == i1, lane == i2).astype(F32)
    before = _dot(tri_ref[...], hot.astype(BF16)) + carry_ref[...]
    r1 = jnp.sum(jnp.where(lane == i1, before, 0.0), axis=-1, keepdims=True)
    r2 = jnp.sum(jnp.where(lane == i2, before, 0.0), axis=-1, keepdims=True)
    carry_ref[...] += jnp.sum(hot, axis=0, keepdims=True)
    cols = (i1, i2, w1, w2, r1, r2)
    info = jnp.zeros((tm, LANES), F32)
    for n, col in enumerate(cols):
        info = jnp.where(lane == n, col, info)
    info_ref[...] = info
    cnt_ref[...] = carry_ref[...]


def _router(h, w_router):
    t, d = h.shape
    tm = ROW_TILE
    wpad = jnp.zeros((d, LANES), F32).at[:, :N_EXPERTS].set(w_router)
    wh = wpad.astype(BF16)
    wl = (wpad - wh.astype(F32)).astype(BF16)
    tri = (np.arange(tm)[:, None] > np.arange(tm)[None, :]).astype(np.float32)
    return pl.pallas_call(
        _router_kernel, grid=(t // tm,),
        in_specs=[_row_spec(tm, d), _full_spec((d, LANES)), _full_spec((d, LANES)), _full_spec((tm, tm))],
        out_specs=[_row_spec(tm, LANES), _full_spec((1, LANES))],
        out_shape=[jax.ShapeDtypeStruct((t, LANES), F32), jax.ShapeDtypeStruct((1, LANES), F32)],
        scratch_shapes=[pltpu.VMEM((1, LANES), F32)],
        compiler_params=_params("arbitrary"), name="router",
    )(h, wh, wl, jnp.asarray(tri, BF16))


def _dispatch_kernel(pos1_ref, pos2_ref, x_ref, init_ref, xs_ref, sem):
    del init_ref
    tm = x_ref.shape[0]
    base = pl.program_id(0) * tm

    def copy(t, pos_ref):
        return pltpu.make_async_copy(x_ref.at[pl.ds(t, 1)], xs_ref.at[pl.ds(pos_ref[base + t], 1)], sem)

    def start(t, carry):
        copy(t, pos1_ref).start()
        copy(t, pos2_ref).start()
        return carry

    def wait(t, carry):
        copy(t, pos1_ref).wait()
        copy(t, pos2_ref).wait()
        return carry

    lax.fori_loop(0, tm, start, 0)
    lax.fori_loop(0, tm, wait, 0)


def _dispatch(h, pos1, pos2, n_slots):
    t, d = h.shape
    tm = ROW_TILE
    grid_spec = pltpu.PrefetchScalarGridSpec(
        num_scalar_prefetch=2, grid=(t // tm,),
        in_specs=[pl.BlockSpec((tm, d), lambda i, p1, p2: (i, 0)), pl.BlockSpec(memory_space=pl.ANY)],
        out_specs=pl.BlockSpec(memory_space=pl.ANY),
        scratch_shapes=[pltpu.SemaphoreType.DMA(())])
    return pl.pallas_call(
        _dispatch_kernel, grid_spec=grid_spec,
        out_shape=jax.ShapeDtypeStruct((n_slots, d), F32),
        input_output_aliases={3: 0},
        compiler_params=_params("arbitrary"), name="moe_dispatch",
    )(pos1, pos2, h, jnp.zeros((n_slots, d), F32))


def _combine_kernel(pos1_ref, pos2_ref, h_ref, w1_ref, w2_ref, ys_ref, g_ref, b_ref, o_ref, buf_ref, sem):
    tm = h_ref.shape[0]
    base = pl.program_id(0) * tm

    def copy(t, k, pos_ref):
        return pltpu.make_async_copy(ys_ref.at[pl.ds(pos_ref[base + t], 1)], buf_ref.at[k, pl.ds(t, 1)], sem)

    def start(t, carry):
        copy(t, 0, pos1_ref).start()
        copy(t, 1, pos2_ref).start()
        return carry

    def wait(t, carry):
        copy(t, 0, pos1_ref).wait()
        copy(t, 1, pos2_ref).wait()
        return carry

    lax.fori_loop(0, tm, start, 0)
    lax.fori_loop(0, tm, wait, 0)
    o_ref[...] = w1_ref[...] * buf_ref[0] + w2_ref[...] * buf_ref[1]
    _residual_ln_inplace(h_ref, g_ref, b_ref, o_ref)


def _combine(h, ys, pos1, pos2, w1, w2, g, b):
    t, d = h.shape
    tm = ROW_TILE
    row = lambda n: pl.BlockSpec((tm, n), lambda i, p1, p2: (i, 0))
    full = lambda s: pl.BlockSpec(s, lambda i, p1, p2: (0, 0))
    grid_spec = pltpu.PrefetchScalarGridSpec(
        num_scalar_prefetch=2, grid=(t // tm,),
        in_specs=[row(d), row(1), row(1), pl.BlockSpec(memory_space=pl.ANY), full((1, d)), full((1, d))],
        out_specs=row(d),
        scratch_shapes=[pltpu.VMEM((2, tm, d), F32), pltpu.SemaphoreType.DMA(())])
    return pl.pallas_call(
        _combine_kernel, grid_spec=grid_spec,
        out_shape=jax.ShapeDtypeStruct((t, d), F32),
        compiler_params=_params("arbitrary"), name="moe_combine",
    )(pos1, pos2, h, w1, w2, ys, g, b)


def _moe_layer(h, w_router, wgu, wd, g, b):
    t, d = h.shape
    tm = ROW_TILE
    info, cnt = _router(h, w_router)
    idx1 = info[:, 0].astype(jnp.int32)
    idx2 = info[:, 1].astype(jnp.int32)
    counts = cnt[0, :N_EXPERTS].astype(jnp.int32)
    padded = ((counts + tm - 1) // tm) * tm
    ends = jnp.cumsum(padded)
    offs = ends - padded
    experts = jnp.arange(N_EXPERTS, dtype=jnp.int32)
    group_start = lambda idx: jnp.sum(jnp.where(idx[:, None] == experts[None, :], offs[None, :], 0), axis=1)
    pos1 = group_start(idx1) + info[:, 4].astype(jnp.int32)
    pos2 = group_start(idx2) + info[:, 5].astype(jnp.int32)
    n_tiles = (2 * t) // tm + N_EXPERTS
    tile_start = jnp.arange(n_tiles, dtype=jnp.int32) * tm
    tile_eid = jnp.minimum(jnp.sum((ends[None, :] <= tile_start[:, None]).astype(jnp.int32), axis=1),
                           N_EXPERTS - 1)
    n_valid = (ends[-1:] // tm).astype(jnp.int32)
    xs = _dispatch(h, pos1, pos2, n_tiles * tm)
    ys = _ffn_grouped(xs, tile_eid, n_valid, wgu, wd)
    return _combine(h, ys, pos1, pos2, info[:, 2:3], info[:, 3:4], g, b)


def _rope_tables(seq):
    tpos = jnp.arange(seq)
    row = (tpos // GRID_W).astype(F32)
    col = (tpos % GRID_W).astype(F32)

    def angles(rot_dim):
        n = rot_dim // 4
        inv = ROPE_THETA ** (-jnp.arange(n, dtype=F32) / n)
        ang = jnp.concatenate([row[:, None] * inv, col[:, None] * inv], axis=-1)
        return jnp.cos(ang), jnp.sin(ang)

    cb, sb = angles(HEAD_DIM)
    cos_b = jnp.concatenate([cb, cb, cb, cb], axis=-1)
    sin_b = jnp.concatenate([-sb, sb, -sb, sb], axis=-1)
    cc, sc = angles(MLA_ROPE)
    ones = jnp.ones((seq, MLA_NOPE), F32)
    z64 = jnp.zeros((seq, MLA_NOPE), F32)
    z32 = jnp.zeros((seq, LANES - MLA_NOPE - MLA_ROPE), F32)
    cos_c = jnp.concatenate([ones, cc, cc, z32], axis=-1)
    sin_c = jnp.concatenate([z64, -sc, sc, z32], axis=-1)
    return cos_b, sin_b, cos_c, sin_c


def _mixer_weights(w_in, gq, gk, w_uq, w_ukv, w_proj_b):
    o = IN_OFFS
    nl, d = w_in.shape[:2]
    half = HEAD_DIM // 2
    w_a = jnp.concatenate([w_in[..., o[0]:o[1]] * (HEAD_DIM ** -0.5), w_in[..., o[1]:o[3]]], axis=-1)
    pair_order = np.array([0, 4, 1, 5, 2, 6, 3, 7])
    wq = w_in[..., o[3]:o[4]].reshape(nl, d, GQA_HEADS, HEAD_DIM)[:, :, pair_order]
    wk = w_in[..., o[4]:o[5]].reshape(nl, d, GQA_KV_HEADS, HEAD_DIM)
    flat = lambda w: w.reshape(w.shape[0], w.shape[1], -1)
    w_b = jnp.concatenate([flat(wq), flat(jnp.roll(wq, half, axis=-1)),
                           flat(wk), flat(jnp.roll(wk, half, axis=-1)),
                           w_in[..., o[5]:o[6]]], axis=-1)
    g2 = lambda g: jnp.tile(g, (1, LANES // HEAD_DIM)).reshape(nl, 1, LANES)
    gains_b = (g2(gq), g2(jnp.roll(gq, half, axis=-1)), g2(gk), g2(jnp.roll(gk, half, axis=-1)))
    wkr = w_in[..., o[8]:o[9]]
    pad_kr = lambda w: jnp.concatenate(
        [jnp.zeros((nl, d, MLA_NOPE), F32), w, jnp.zeros((nl, d, LANES - MLA_NOPE - MLA_ROPE), F32)], axis=-1)
    w_c = jnp.concatenate([w_in[..., o[6]:o[8]], pad_kr(wkr), pad_kr(jnp.roll(wkr, MLA_ROPE // 2, axis=-1))],
                          axis=-1)
    uq = w_uq.reshape(nl, MLA_Q_RANK, MLA_HEADS, MLA_NOPE + MLA_ROPE)
    zq = lambda n: jnp.zeros((nl, MLA_Q_RANK, MLA_HEADS, n), F32)
    q_pad = jnp.concatenate([uq, zq(LANES - MLA_NOPE - MLA_ROPE)], axis=-1)
    q_rot = jnp.concatenate([zq(MLA_NOPE), jnp.roll(uq[..., MLA_NOPE:], MLA_ROPE // 2, axis=-1),
                             zq(LANES - MLA_NOPE - MLA_ROPE)], axis=-1)
    wuq = jnp.concatenate([flat(q_pad), flat(q_rot)], axis=-1)
    ukv = w_ukv.reshape(nl, MLA_KV_RANK, MLA_HEADS, MLA_NOPE + MLA_V)
    k_part = jnp.concatenate(
        [ukv[..., :MLA_NOPE], jnp.zeros((nl, MLA_KV_RANK, MLA_HEADS, LANES - MLA_NOPE), F32)], axis=-1)
    wukv = jnp.concatenate([flat(k_part), flat(ukv[..., MLA_NOPE:])], axis=-1)
    wpb = w_proj_b.reshape(nl, GQA_HEADS, HEAD_DIM, -1)[:, pair_order].reshape(nl, B_WIDTH, -1)
    bf = lambda w: w.astype(BF16)
    return bf(w_a), bf(w_b), gains_b, bf(w_c), bf(wuq), bf(wukv), bf(w_in[..., o[9]:]), bf(wpb)


def kernel(x, mem, emb_ln_g, emb_ln_b, w_in, na_rpb, gqa_q_norm, gqa_k_norm, mla_q_norm, mla_kv_norm,
           mla_w_uq, mla_w_ukv, w_proj_a, w_proj_b, w_proj_c, w_out, mem_wq, mem_wkv, mem_wo, ln_g, ln_b,
           ffn_w_gu, ffn_w_down, moe_router, moe_w_gu, moe_w_down):
    batch, seq, d = x.shape
    t = batch * seq
    cos_b, sin_b, cos_c, sin_c = _rope_tables(seq)
    seg = np.kron(np.eye(LANES // HEAD_DIM), np.full((HEAD_DIM, HEAD_DIM), 1.0 / HEAD_DIM))
    seg = jnp.asarray(seg, BF16)
    mem2 = mem.reshape(-1, d)
    vec = lambda v: v.reshape(1, -1)
    bf = lambda w: w.astype(BF16)

    w_a, w_b, gains_b, w_c, wuq, wukv, w_g, wpb = _mixer_weights(
        w_in, gqa_q_norm, gqa_k_norm, mla_w_uq, mla_w_ukv, w_proj_b)
    na_bias = _na_bias_tiles(na_rpb, seq // GRID_W)
    wpa, wpc, wo, mwq, mwkv, mwo = (bf(w) for w in (w_proj_a, w_proj_c, w_out, mem_wq, mem_wkv, mem_wo))
    ffn_gu, ffn_d, moe_gu, moe_d = (bf(w) for w in (ffn_w_gu, ffn_w_down, moe_w_gu, moe_w_down))

    h = _layer_norm(x.reshape(t, d), emb_ln_g, emb_ln_b)
    for l in range(DEPTH):
        qkv_a = _proj(h, w_a[l], name="proj_a")
        ya = _neighborhood_attention(qkv_a, na_bias[l], batch, seq)
        qb, kb, vb = _b_prologue(h, w_b[l], *(g[l] for g in gains_b), cos_b, sin_b, seg, seq)
        yb = _paired_attention(qb, kb, vb, batch, seq, shared=True)
        qc, kc, vc = _c_prologue(h, w_c[l], vec(mla_q_norm[l]), vec(mla_kv_norm[l]), wuq[l], wukv[l],
                                 cos_c, sin_c, seq)
        yc = _paired_attention(qc, kc, vc, batch, seq, shared=False)
        gates = _proj(h, w_g[l], act="sigmoid", name="proj_gates")
        h = _merge_out(ya, yb, yc, gates, h, wpa[l], wpb[l], wpc[l], wo[l], vec(ln_g[l, 0]), vec(ln_b[l, 0]))
        kv = _proj(mem2, mwkv[l], tm=256, name="proj_mem_kv")
        h = _cross_attention(h, kv, mwq[l], mwo[l], vec(ln_g[l, 1]), vec(ln_b[l, 1]), batch, seq)
        if l % 2 == 0:
            h = _ffn_dense(h, ffn_gu[l // 2:l // 2 + 1], ffn_d[l // 2:l // 2 + 1],
                           vec(ln_g[l, 2]), vec(ln_b[l, 2]))
        else:
            h = _moe_layer(h, moe_router[l // 2], moe_gu[l // 2], moe_d[l // 2],
                           vec(ln_g[l, 2]), vec(ln_b[l, 2]))
    return h.reshape(batch, seq, d)
```

```python
import functools

import jax
import jax.numpy as jnp
import numpy as np
from jax import lax
from jax.experimental import pallas as pl
from jax.experimental.pallas import tpu as pltpu

F32 = jnp.float32
BF16 = jnp.bfloat16

D_MODEL = 1024
DEPTH = 4
GRID_W = 64
HEAD_DIM = 64
ROPE_THETA = 10000.0
LN_EPS = 1e-5
RMS_EPS = 1e-6
NA_HEADS = 4
NA_ROWS = 8
NA_COLS = 16
GQA_HEADS = 8
GQA_KV_HEADS = 2
MLA_HEADS = 4
MLA_Q_RANK = 256
MLA_KV_RANK = 128
MLA_NOPE = 64
MLA_ROPE = 32
MLA_V = 64
A_WIDTH = NA_HEADS * HEAD_DIM
B_WIDTH = GQA_HEADS * HEAD_DIM
C_WIDTH = MLA_HEADS * MLA_V
IN_SPLITS = (A_WIDTH, A_WIDTH, A_WIDTH,
             B_WIDTH, GQA_KV_HEADS * HEAD_DIM, GQA_KV_HEADS * HEAD_DIM,
             MLA_Q_RANK, MLA_KV_RANK, MLA_ROPE,
             3 * D_MODEL)
IN_OFFS = tuple(int(v) for v in np.cumsum((0,) + IN_SPLITS))
MEM_HEADS = 4
MEM_HEAD_DIM = D_MODEL // MEM_HEADS
D_FF = ((8 * D_MODEL // 3 + 127) // 128) * 128
N_EXPERTS = 8
ALPHA = (2 * DEPTH) ** 0.25

LANES = 128
FF_CHUNK = 256
ROW_TILE = 512
ATT_TQ = 512
NA_ROWS_PER_STEP = 8
DMA_UNROLL = 8
NEG_BIG = -1e30
LOG2E = 1.4426950408889634
ROW_L0 = HEAD_DIM
ROW_L1 = 0
VMEM_LIMIT = 56 * 1024 * 1024


def _params(*sem):
    return pltpu.CompilerParams(dimension_semantics=sem, vmem_limit_bytes=VMEM_LIMIT)


def _dot(a, b):
    return jnp.dot(a, b, preferred_element_type=F32)


def _dot_nt(a, b):
    return lax.dot_general(a, b, (((1,), (1,)), ((), ())), preferred_element_type=F32)


def _ln_rows(y, g, b):
    mu = jnp.mean(y, axis=-1, keepdims=True)
    d = y - mu
    var = jnp.mean(d * d, axis=-1, keepdims=True)
    return d * lax.rsqrt(var + LN_EPS) * g + b


def _row_spec(tm, n):
    return pl.BlockSpec((tm, n), lambda i: (i, 0))


def _full_spec(shape):
    nd = len(shape)
    return pl.BlockSpec(shape, lambda *_: (0,) * nd)


def _ln_kernel(x_ref, g_ref, b_ref, o_ref):
    o_ref[...] = _ln_rows(x_ref[...], g_ref[...], b_ref[...])


def _layer_norm(x, g, b):
    t, d = x.shape
    return pl.pallas_call(
        _ln_kernel, grid=(t // ROW_TILE,),
        in_specs=[_row_spec(ROW_TILE, d), _full_spec((1, d)), _full_spec((1, d))],
        out_specs=_row_spec(ROW_TILE, d),
        out_shape=jax.ShapeDtypeStruct((t, d), F32),
        compiler_params=_params("parallel"), name="embed_ln",
    )(x, g.reshape(1, d), b.reshape(1, d))


def _proj_kernel(x_ref, w_ref, o_ref, *, act, n_chunk):
    x = x_ref[...].astype(BF16)
    n = w_ref.shape[1]
    for c in range(0, n, n_chunk):
        e = min(c + n_chunk, n)
        y = _dot(x, w_ref[:, c:e])
        if act == "sigmoid":
            y = jax.nn.sigmoid(y)
        o_ref[:, c:e] = y.astype(o_ref.dtype)


def _proj(x, w, act=None, tm=ROW_TILE, n_chunk=512, name="proj"):
    t, k = x.shape
    n = w.shape[1]
    n_chunk = min(n_chunk, n)
    return pl.pallas_call(
        functools.partial(_proj_kernel, act=act, n_chunk=n_chunk), grid=(t // tm,),
        in_specs=[_row_spec(tm, k), _full_spec((k, n))],
        out_specs=_row_spec(tm, n),
        out_shape=jax.ShapeDtypeStruct((t, n), BF16),
        compiler_params=_params("parallel"), name=name,
    )(x, w)


def _bpro_kernel(x_ref, w_ref, gq_ref, gqr_ref, gk_ref, gkr_ref, cos_ref, sin_ref, seg_ref,
                 q_ref, k_ref, v_ref):
    x = x_ref[...].astype(BF16)
    cos = cos_ref[...]
    sin = sin_ref[...]
    seg = seg_ref[...]

    def norm_rope(z, zr, g, gr):
        ms = _dot((z * z).astype(BF16), seg)
        return (z * g * cos + zr * gr * sin) * lax.rsqrt(ms + RMS_EPS)

    nq = B_WIDTH
    z = _dot(x, w_ref[:, :2 * nq])
    for j in range(nq // LANES):
        lo, hi = j * LANES, (j + 1) * LANES
        q = norm_rope(z[:, lo:hi], z[:, nq + lo:nq + hi], gq_ref[...], gqr_ref[...])
        q_ref[:, lo:hi] = (q * (HEAD_DIM ** -0.5 * LOG2E)).astype(BF16)
    zk = _dot(x, w_ref[:, 2 * nq:])
    k_ref[...] = norm_rope(zk[:, :LANES], zk[:, LANES:2 * LANES], gk_ref[...], gkr_ref[...]).astype(BF16)
    v_ref[0] = _value_rows(zk[:, 2 * LANES:])


def _b_prologue(h, w, gq, gqr, gk, gkr, cos, sin, seg, seq):
    t, d = h.shape
    tm = ROW_TILE
    nsb = seq // tm
    tab = pl.BlockSpec((tm, LANES), lambda i: (i % nsb, 0))
    vec = _full_spec((1, LANES))
    return pl.pallas_call(
        _bpro_kernel, grid=(t // tm,),
        in_specs=[_row_spec(tm, d), _full_spec(w.shape), vec, vec, vec, vec, tab, tab,
                  _full_spec((LANES, LANES))],
        out_specs=[_row_spec(tm, B_WIDTH), _row_spec(tm, LANES),
                   pl.BlockSpec((1, 2 * LANES, tm), lambda i: (i, 0, 0))],
        out_shape=[jax.ShapeDtypeStruct((t, B_WIDTH), BF16),
                   jax.ShapeDtypeStruct((t, LANES), BF16),
                   jax.ShapeDtypeStruct((t // tm, 2 * LANES, tm), BF16)],
        compiler_params=_params("parallel"), name="b_prologue",
    )(h, w, gq, gqr, gk, gkr, cos, sin, seg)


def _cpro_kernel(x_ref, w_ref, gq_ref, gkv_ref, wuq_ref, wukv_ref, cos_ref, sin_ref,
                 q_ref, k_ref, v_ref):
    x = x_ref[...].astype(BF16)
    cos = cos_ref[...]
    sin = sin_ref[...]
    z = _dot(x, w_ref[...])
    nh = MLA_HEADS
    scale = (MLA_NOPE + MLA_ROPE) ** -0.5 * LOG2E

    def rms(v, g):
        return (v * lax.rsqrt(jnp.mean(v * v, axis=-1, keepdims=True) + RMS_EPS) * g).astype(BF16)

    qq = _dot(rms(z[:, :MLA_Q_RANK], gq_ref[...]), wuq_ref[...])
    for hd in range(nh):
        lo, hi = hd * LANES, (hd + 1) * LANES
        q = qq[:, lo:hi] * cos + qq[:, nh * LANES + lo:nh * LANES + hi] * sin
        q_ref[:, lo:hi] = (q * scale).astype(BF16)
    o = MLA_Q_RANK + MLA_KV_RANK
    kk = _dot(rms(z[:, MLA_Q_RANK:o], gkv_ref[...]), wukv_ref[...])
    kpe = z[:, o:o + LANES] * cos + z[:, o + LANES:o + 2 * LANES] * sin
    for hd in range(nh):
        lo, hi = hd * LANES, (hd + 1) * LANES
        k_ref[:, lo:hi] = (kk[:, lo:hi] + kpe).astype(BF16)
    for j in range(C_WIDTH // LANES):
        vj = kk[:, (nh + j) * LANES:(nh + j + 1) * LANES]
        v_ref[0, 2 * j * LANES:2 * (j + 1) * LANES, :] = _value_rows(vj)


def _c_prologue(h, w, gq, gkv, wuq, wukv, cos, sin, seq):
    t, d = h.shape
    tm = ROW_TILE
    nsb = seq // tm
    tab = pl.BlockSpec((tm, LANES), lambda i: (i % nsb, 0))
    nq = MLA_HEADS * LANES
    return pl.pallas_call(
        _cpro_kernel, grid=(t // tm,),
        in_specs=[_row_spec(tm, d), _full_spec(w.shape), _full_spec(gq.shape), _full_spec(gkv.shape),
                  _full_spec(wuq.shape), _full_spec(wukv.shape), tab, tab],
        out_specs=[_row_spec(tm, nq), _row_spec(tm, nq),
                   pl.BlockSpec((1, 2 * C_WIDTH, tm), lambda i: (i, 0, 0))],
        out_shape=[jax.ShapeDtypeStruct((t, nq), BF16),
                   jax.ShapeDtypeStruct((t, nq), BF16),
                   jax.ShapeDtypeStruct((t // tm, 2 * C_WIDTH, tm), BF16)],
        compiler_params=_params("parallel"), name="c_prologue",
    )(h, w, gq, gkv, wuq, wukv, cos, sin)


def _attn_kernel(q_ref, k_ref, vt_ref, o_ref, m_ref, mx_ref, acc_ref, st0_ref, st1_ref, *, shared):
    tq = o_ref.shape[0]
    n_chunks, _, tk = vt_ref.shape
    if shared:
        left = lax.broadcasted_iota(jnp.int32, (tq, LANES), 1) < HEAD_DIM
        q2 = q_ref[...]
        zero = jnp.zeros_like(q2)
        qs = (jnp.where(left, q2, zero), jnp.where(left, zero, q2))
    else:
        qs = (q_ref[:, :LANES], q_ref[:, LANES:])
    m_ref[...] = jnp.full(m_ref.shape, NEG_BIG, F32)
    acc_ref[...] = jnp.zeros(acc_ref.shape, F32)

    def scores(c, st_ref):
        off = pl.multiple_of(c * tk, tk)
        for a in range(2):
            if shared:
                kc = k_ref[pl.ds(off, tk), :]
            else:
                kc = k_ref[pl.ds(off, tk), a * LANES:(a + 1) * LANES]
            st = _dot_nt(kc, qs[a])
            st_ref[a] = st
            mx_ref[c % 2, a] = jnp.max(st, axis=0, keepdims=True)

    def consume(c, st_ref):
        for a in range(2):
            st = st_ref[a]
            m_prev = m_ref[a]
            m_new = jnp.maximum(m_prev, mx_ref[c % 2, a])
            alpha = jnp.exp2(m_prev - m_new)
            p = jnp.exp2(st - m_new).astype(BF16)
            acc_ref[a] = alpha * acc_ref[a] + _dot(vt_ref[c, a * LANES:(a + 1) * LANES, :], p)
            m_ref[a] = m_new

    bufs = (st0_ref, st1_ref)
    scores(0, st0_ref)
    for c in range(n_chunks):
        if c + 1 < n_chunks:
            scores(c + 1, bufs[(c + 1) % 2])
        consume(c, bufs[c % 2])
    top = lax.broadcasted_iota(jnp.int32, (LANES, tq), 0) < HEAD_DIM
    acc0 = acc_ref[0]
    acc1 = acc_ref[1]
    ot = jnp.where(top, acc0 / acc0[ROW_L0:ROW_L0 + 1], acc1 / acc1[ROW_L1:ROW_L1 + 1])
    o_ref[...] = ot.T.astype(o_ref.dtype)


def _value_rows(v):
    vt = v.T
    row = lax.broadcasted_iota(jnp.int32, vt.shape, 0)
    v0 = jnp.where(row < HEAD_DIM, vt, jnp.where(row == ROW_L0, 1.0, 0.0))
    v1 = jnp.where(row >= HEAD_DIM, vt, jnp.where(row == ROW_L1, 1.0, 0.0))
    return jnp.concatenate([v0, v1], axis=0).astype(BF16)


def _paired_attention(q, k, vt, batch, seq, shared):
    t = q.shape[0]
    tk = vt.shape[2]
    n_pairs = vt.shape[1] // (2 * LANES) if not shared else q.shape[1] // LANES
    tq = ATT_TQ
    nqb = seq // tq
    nkc = seq // tk
    qw = LANES if shared else 2 * LANES
    q_spec = pl.BlockSpec((tq, qw), lambda b, j, i: (b * nqb + i, j))
    if shared:
        k_spec = pl.BlockSpec((seq, LANES), lambda b, j, i: (b, 0))
        v_spec = pl.BlockSpec((nkc, 2 * LANES, tk), lambda b, j, i: (b, 0, 0))
    else:
        k_spec = pl.BlockSpec((seq, 2 * LANES), lambda b, j, i: (b, j))
        v_spec = pl.BlockSpec((nkc, 2 * LANES, tk), lambda b, j, i: (b, j, 0))
    return pl.pallas_call(
        functools.partial(_attn_kernel, shared=shared),
        grid=(batch, n_pairs, nqb),
        in_specs=[q_spec, k_spec, v_spec],
        out_specs=pl.BlockSpec((tq, LANES), lambda b, j, i: (b * nqb + i, j)),
        out_shape=jax.ShapeDtypeStruct((t, n_pairs * LANES), BF16),
        scratch_shapes=[pltpu.VMEM((2, 1, tq), F32), pltpu.VMEM((2, 2, 1, tq), F32),
                        pltpu.VMEM((2, LANES, tq), F32),
                        pltpu.VMEM((2, tk, tq), F32), pltpu.VMEM((2, tk, tq), F32)],
        compiler_params=_params("parallel", "parallel", "arbitrary"),
        name="attn_shared" if shared else "attn_split",
    )(q, k, vt)


def _na_kernel(q_ref, k_ref, v_ref, bias_ref, o_ref, *, rows_per_step):
    seq = k_ref.shape[0]
    rows = seq // GRID_W
    n_win = NA_ROWS * GRID_W
    left = lax.broadcasted_iota(jnp.int32, (GRID_W, LANES), 1) < HEAD_DIM

    def row_body(i, carry):
        r = pl.program_id(1) * rows_per_step + i
        r0 = jnp.clip(r - NA_ROWS // 2, 0, rows - NA_ROWS)
        rho = r - r0
        qoff = pl.multiple_of(i * GRID_W, GRID_W)
        koff = pl.multiple_of(r0 * GRID_W, GRID_W)
        heads = [(j, a) for j in range(A_WIDTH // LANES) for a in range(2)]
        scores = []
        for j, a in heads:
            qj = q_ref[pl.ds(qoff, GRID_W), j * LANES:(j + 1) * LANES]
            kj = k_ref[pl.ds(koff, n_win), j * LANES:(j + 1) * LANES]
            zero = jnp.zeros_like(qj)
            qm = jnp.where(left, qj, zero) if a == 0 else jnp.where(left, zero, qj)
            scores.append(_dot_nt(qm, kj) + bias_ref[2 * j + a, rho])
        probs = [jnp.exp(s - jnp.max(s, axis=-1, keepdims=True)) for s in scores]
        invs = [1.0 / jnp.sum(p, axis=-1, keepdims=True) for p in probs]
        outs = []
        for (j, a), p, inv in zip(heads, probs, invs):
            vj = v_ref[pl.ds(koff, n_win), j * LANES:(j + 1) * LANES]
            outs.append(_dot(p.astype(BF16), vj) * inv)
        merged = [jnp.where(left, outs[2 * j], outs[2 * j + 1]) for j in range(A_WIDTH // LANES)]
        o_ref[pl.ds(qoff, GRID_W), :] = jnp.concatenate(merged, axis=-1).astype(o_ref.dtype)
        return carry

    lax.fori_loop(0, rows_per_step, row_body, 0, unroll=2)


def _neighborhood_attention(qkv, bias, batch, seq):
    t = qkv.shape[0]
    rows = seq // GRID_W
    rps = NA_ROWS_PER_STEP
    tq = rps * GRID_W
    nqb = rows // rps
    return pl.pallas_call(
        functools.partial(_na_kernel, rows_per_step=rps),
        grid=(batch, nqb),
        in_specs=[pl.BlockSpec((tq, A_WIDTH), lambda b, i: (b * nqb + i, 0)),
                  pl.BlockSpec((seq, A_WIDTH), lambda b, i: (b, 1)),
                  pl.BlockSpec((seq, A_WIDTH), lambda b, i: (b, 2)),
                  _full_spec(bias.shape)],
        out_specs=pl.BlockSpec((tq, A_WIDTH), lambda b, i: (b * nqb + i, 0)),
        out_shape=jax.ShapeDtypeStruct((t, A_WIDTH), BF16),
        compiler_params=_params("parallel", "arbitrary"), name="nbr_attn",
    )(qkv, qkv, qkv, bias)


def _na_bias_tiles(rpb, rows):
    c = np.arange(GRID_W)
    c0 = np.clip(c - NA_COLS // 2, 0, GRID_W - NA_COLS)
    kc = np.arange(GRID_W)
    in_win = (kc[None, :] >= c0[:, None]) & (kc[None, :] < c0[:, None] + NA_COLS)
    dc = kc[None, :] - c[:, None] + (NA_COLS - 1)
    pick = (dc[None] == np.arange(2 * NA_COLS - 1)[:, None, None]) & in_win[None]
    lead = rpb.shape[:-2]
    toep = jnp.einsum("hrd,dck->hrck", rpb.reshape((-1,) + rpb.shape[-2:]), jnp.asarray(pick, F32),
                      precision=lax.Precision.HIGHEST)
    toep = jnp.where(in_win[None, None], toep, NEG_BIG)
    kr = min(NA_ROWS, rows)
    per_rho = [toep[:, NA_ROWS - 1 - rho:NA_ROWS - 1 - rho + kr] for rho in range(kr)]
    tiles = jnp.stack(per_rho, axis=1)
    tiles = tiles.transpose(0, 1, 3, 2, 4).reshape(lead + (kr, GRID_W, kr * GRID_W))
    return tiles.astype(F32)


def _merge_kernel(ya_ref, yb_ref, yc_ref, gate_ref, h_ref, wa_ref, wb_ref, wc_ref, wo_ref,
                  g_ref, b_ref, o_ref):
    d = D_MODEL
    merged = (gate_ref[:, :d].astype(F32) * _dot(ya_ref[...], wa_ref[...])
              + gate_ref[:, d:2 * d].astype(F32) * _dot(yb_ref[...], wb_ref[...])
              + gate_ref[:, 2 * d:].astype(F32) * _dot(yc_ref[...], wc_ref[...]))
    o_ref[...] = _dot(merged.astype(BF16), wo_ref[...])
    _residual_ln_inplace(h_ref, g_ref, b_ref, o_ref)


def _merge_out(ya, yb, yc, gates, h, wa, wb, wc, wo, g, b):
    t, d = h.shape
    tm = ROW_TILE
    return pl.pallas_call(
        _merge_kernel, grid=(t // tm,),
        in_specs=[_row_spec(tm, ya.shape[1]), _row_spec(tm, yb.shape[1]), _row_spec(tm, yc.shape[1]),
                  _row_spec(tm, 3 * d), _row_spec(tm, d),
                  _full_spec(wa.shape), _full_spec(wb.shape), _full_spec(wc.shape), _full_spec(wo.shape),
                  _full_spec((1, d)), _full_spec((1, d))],
        out_specs=_row_spec(tm, d),
        out_shape=jax.ShapeDtypeStruct((t, d), F32),
        compiler_params=_params("parallel"), name="merge_out",
    )(ya, yb, yc, gates, h, wa, wb, wc, wo, g, b)


def _cross_kernel(h_ref, k_ref, v_ref, wq_ref, wo_ref, g_ref, b_ref, o_ref):
    q = (_dot(h_ref[...].astype(BF16), wq_ref[...]) * (MEM_HEAD_DIM ** -0.5)).astype(BF16)
    outs = []
    for hd in range(MEM_HEADS):
        lo, hi = hd * MEM_HEAD_DIM, (hd + 1) * MEM_HEAD_DIM
        s = _dot_nt(q[:, lo:hi], k_ref[:, lo:hi])
        p = jnp.exp(s - jnp.max(s, axis=-1, keepdims=True))
        inv = 1.0 / jnp.sum(p, axis=-1, keepdims=True)
        outs.append((_dot(p.astype(BF16), v_ref[:, lo:hi]) * inv).astype(BF16))
    o_ref[...] = _dot(jnp.concatenate(outs, axis=-1), wo_ref[...])
    _residual_ln_inplace(h_ref, g_ref, b_ref, o_ref)


def _cross_attention(h, kv, wq, wo, g, b, batch, seq):
    t, d = h.shape
    tm = ROW_TILE
    nsb = seq // tm
    m = kv.shape[0] // batch
    return pl.pallas_call(
        _cross_kernel, grid=(batch, nsb),
        in_specs=[pl.BlockSpec((tm, d), lambda bi, i: (bi * nsb + i, 0)),
                  pl.BlockSpec((m, d), lambda bi, i: (bi, 0)),
                  pl.BlockSpec((m, d), lambda bi, i: (bi, 1)),
                  _full_spec(wq.shape), _full_spec(wo.shape), _full_spec((1, d)), _full_spec((1, d))],
        out_specs=pl.BlockSpec((tm, d), lambda bi, i: (bi * nsb + i, 0)),
        out_shape=jax.ShapeDtypeStruct((t, d), F32),
        compiler_params=_params("parallel", "arbitrary"), name="cross_attn",
    )(h, kv, kv, wq, wo, g, b)


def _swiglu(x, wgu_ref, wd_ref, act_ref):
    for c in range(0, D_FF, FF_CHUNK):
        a = _dot(x, wgu_ref[0, :, c:c + FF_CHUNK])
        u = _dot(x, wgu_ref[0, :, D_FF + c:D_FF + c + FF_CHUNK])
        act_ref[:, c:c + FF_CHUNK] = (a * jax.nn.sigmoid(a) * u).astype(BF16)
    return _dot(act_ref[...], wd_ref[0])


def _residual_ln_inplace(h_ref, g_ref, b_ref, o_ref):
    o_ref[...] = _ln_rows(ALPHA * h_ref[...] + o_ref[...], g_ref[...], b_ref[...])


def _ffn_dense_kernel(h_ref, wgu_ref, wd_ref, g_ref, b_ref, o_ref, act_ref):
    o_ref[...] = _swiglu(h_ref[...].astype(BF16), wgu_ref, wd_ref, act_ref)
    _residual_ln_inplace(h_ref, g_ref, b_ref, o_ref)


def _ffn_dense(h, wgu, wd, g, b):
    t, d = h.shape
    tm = ROW_TILE
    wspec = lambda w: pl.BlockSpec((1,) + w.shape[1:], lambda i: (0, 0, 0))
    return pl.pallas_call(
        _ffn_dense_kernel, grid=(t // tm,),
        in_specs=[_row_spec(tm, d), wspec(wgu), wspec(wd), _full_spec((1, d)), _full_spec((1, d))],
        out_specs=_row_spec(tm, d),
        out_shape=jax.ShapeDtypeStruct((t, d), F32),
        scratch_shapes=[pltpu.VMEM((tm, D_FF), BF16)],
        compiler_params=_params("parallel"), name="ffn_dense",
    )(h, wgu, wd, g, b)


def _ffn_grouped_kernel(eid_ref, nvalid_ref, x_ref, wgu_ref, wd_ref, o_ref, act_ref):
    i = pl.program_id(0)

    @pl.when(i < nvalid_ref[0])
    def _():
        o_ref[...] = _swiglu(x_ref[...].astype(BF16), wgu_ref, wd_ref, act_ref)

    @pl.when(i >= nvalid_ref[0])
    def _():
        o_ref[...] = jnp.zeros(o_ref.shape, F32)


def _ffn_grouped(xs, tile_eid, n_valid, wgu, wd):
    p, d = xs.shape
    tm = ROW_TILE
    wspec = lambda w: pl.BlockSpec((1,) + w.shape[1:], lambda i, eid, nv: (eid[i], 0, 0))
    grid_spec = pltpu.PrefetchScalarGridSpec(
        num_scalar_prefetch=2, grid=(p // tm,),
        in_specs=[pl.BlockSpec((tm, d), lambda i, eid, nv: (i, 0)), wspec(wgu), wspec(wd)],
        out_specs=pl.BlockSpec((tm, d), lambda i, eid, nv: (i, 0)),
        scratch_shapes=[pltpu.VMEM((tm, D_FF), BF16)])
    return pl.pallas_call(
        _ffn_grouped_kernel, grid_spec=grid_spec,
        out_shape=jax.ShapeDtypeStruct((p, d), F32),
        compiler_params=_params("arbitrary"), name="ffn_grouped",
    )(tile_eid, n_valid, xs, wgu, wd)


def _router_kernel(h_ref, wh_ref, wl_ref, tri_ref, info_ref, cnt_ref, carry_ref):
    i = pl.program_id(0)

    @pl.when(i == 0)
    def _():
        carry_ref[...] = jnp.zeros(carry_ref.shape, F32)

    x = h_ref[...]
    xh = x.astype(BF16)
    xl = (x - xh.astype(F32)).astype(BF16)
    logits = _dot(xh, wh_ref[...]) + (_dot(xh, wl_ref[...]) + _dot(xl, wh_ref[...]))
    tm = x.shape[0]
    lane = lax.broadcasted_iota(jnp.int32, (tm, LANES), 1).astype(F32)
    lg = jnp.where(lane < N_EXPERTS, logits, NEG_BIG)
    m1 = jnp.max(lg, axis=-1, keepdims=True)
    i1 = jnp.min(jnp.where(lg == m1, lane, float(LANES)), axis=-1, keepdims=True)
    lg2 = jnp.where(lane == i1, NEG_BIG, lg)
    m2 = jnp.max(lg2, axis=-1, keepdims=True)
    i2 = jnp.min(jnp.where(lg2 == m2, lane, float(LANES)), axis=-1, keepdims=True)
    e = jnp.exp(m2 - m1)
    w1 = 1.0 / (1.0 + e)
    w2 = e * w1
    hot = jnp.logical_or(lane == i1, lane == i2).astype(F32)
    before = _dot(tri_ref[...], hot.astype(BF16)) + carry_ref[...]
    r1 = jnp.sum(jnp.where(lane == i1, before, 0.0), axis=-1, keepdims=True)
    r2 = jnp.sum(jnp.where(lane == i2, before, 0.0), axis=-1, keepdims=True)
    carry_ref[...] += jnp.sum(hot, axis=0, keepdims=True)
    cols = (i1, i2, w1, w2, r1, r2)
    info = jnp.zeros((tm, LANES), F32)
    for n, col in enumerate(cols):
        info = jnp.where(lane == n, col, info)
    info_ref[...] = info
    cnt_ref[...] = carry_ref[...]


def _router(h, w_router):
    t, d = h.shape
    tm = ROW_TILE
    wpad = jnp.zeros((d, LANES), F32).at[:, :N_EXPERTS].set(w_router)
    wh = wpad.astype(BF16)
    wl = (wpad - wh.astype(F32)).astype(BF16)
    tri = (np.arange(tm)[:, None] > np.arange(tm)[None, :]).astype(np.float32)
    return pl.pallas_call(
        _router_kernel, grid=(t // tm,),
        in_specs=[_row_spec(tm, d), _full_spec((d, LANES)), _full_spec((d, LANES)), _full_spec((tm, tm))],
        out_specs=[_row_spec(tm, LANES), _full_spec((1, LANES))],
        out_shape=[jax.ShapeDtypeStruct((t, LANES), F32), jax.ShapeDtypeStruct((1, LANES), F32)],
        scratch_shapes=[pltpu.VMEM((1, LANES), F32)],
        compiler_params=_params("arbitrary"), name="router",
    )(h, wh, wl, jnp.asarray(tri, BF16))


def _dispatch_kernel(pos1_ref, pos2_ref, x_ref, init_ref, xs_ref, sem):
    del init_ref
    tm = x_ref.shape[0]
    base = pl.program_id(0) * tm

    def copy(t, pos_ref):
        return pltpu.make_async_copy(x_ref.at[pl.ds(t, 1)], xs_ref.at[pl.ds(pos_ref[base + t], 1)], sem)

    def start(t, carry):
        copy(t, pos1_ref).start()
        copy(t, pos2_ref).start()
        return carry

    def wait(t, carry):
        copy(t, pos1_ref).wait()
        copy(t, pos2_ref).wait()
        return carry

    lax.fori_loop(0, tm, start, 0, unroll=DMA_UNROLL)
    lax.fori_loop(0, tm, wait, 0, unroll=DMA_UNROLL)


def _dispatch(h, pos1, pos2, n_slots):
    t, d = h.shape
    tm = ROW_TILE
    grid_spec = pltpu.PrefetchScalarGridSpec(
        num_scalar_prefetch=2, grid=(t // tm,),
        in_specs=[pl.BlockSpec((tm, d), lambda i, p1, p2: (i, 0)), pl.BlockSpec(memory_space=pl.ANY)],
        out_specs=pl.BlockSpec(memory_space=pl.ANY),
        scratch_shapes=[pltpu.SemaphoreType.DMA(())])
    return pl.pallas_call(
        _dispatch_kernel, grid_spec=grid_spec,
        out_shape=jax.ShapeDtypeStruct((n_slots, d), F32),
        input_output_aliases={3: 0},
        compiler_params=_params("arbitrary"), name="moe_dispatch",
    )(pos1, pos2, h, jnp.zeros((n_slots, d), F32))


def _combine_kernel(pos1_ref, pos2_ref, h_ref, w1_ref, w2_ref, ys_ref, g_ref, b_ref, o_ref, buf_ref, sem):
    tm = h_ref.shape[0]
    base = pl.program_id(0) * tm

    def copy(t, k, pos_ref):
        return pltpu.make_async_copy(ys_ref.at[pl.ds(pos_ref[base + t], 1)], buf_ref.at[k, pl.ds(t, 1)], sem)

    def start(t, carry):
        copy(t, 0, pos1_ref).start()
        copy(t, 1, pos2_ref).start()
        return carry

    def wait(t, carry):
        copy(t, 0, pos1_ref).wait()
        copy(t, 1, pos2_ref).wait()
        return carry

    lax.fori_loop(0, tm, start, 0, unroll=DMA_UNROLL)
    lax.fori_loop(0, tm, wait, 0, unroll=DMA_UNROLL)
    o_ref[...] = w1_ref[...] * buf_ref[0] + w2_ref[...] * buf_ref[1]
    _residual_ln_inplace(h_ref, g_ref, b_ref, o_ref)


def _combine(h, ys, pos1, pos2, w1, w2, g, b):
    t, d = h.shape
    tm = ROW_TILE
    row = lambda n: pl.BlockSpec((tm, n), lambda i, p1, p2: (i, 0))
    full = lambda s: pl.BlockSpec(s, lambda i, p1, p2: (0, 0))
    grid_spec = pltpu.PrefetchScalarGridSpec(
        num_scalar_prefetch=2, grid=(t // tm,),
        in_specs=[row(d), row(1), row(1), pl.BlockSpec(memory_space=pl.ANY), full((1, d)), full((1, d))],
        out_specs=row(d),
        scratch_shapes=[pltpu.VMEM((2, tm, d), F32), pltpu.SemaphoreType.DMA(())])
    return pl.pallas_call(
        _combine_kernel, grid_spec=grid_spec,
        out_shape=jax.ShapeDtypeStruct((t, d), F32),
        compiler_params=_params("arbitrary"), name="moe_combine",
    )(pos1, pos2, h, w1, w2, ys, g, b)


def _moe_layer(h, w_router, wgu, wd, g, b):
    t, d = h.shape
    tm = ROW_TILE
    info, cnt = _router(h, w_router)
    idx1 = info[:, 0].astype(jnp.int32)
    idx2 = info[:, 1].astype(jnp.int32)
    counts = cnt[0, :N_EXPERTS].astype(jnp.int32)
    padded = ((counts + tm - 1) // tm) * tm
    ends = jnp.cumsum(padded)
    offs = ends - padded
    experts = jnp.arange(N_EXPERTS, dtype=jnp.int32)
    group_start = lambda idx: jnp.sum(jnp.where(idx[:, None] == experts[None, :], offs[None, :], 0), axis=1)
    pos1 = group_start(idx1) + info[:, 4].astype(jnp.int32)
    pos2 = group_start(idx2) + info[:, 5].astype(jnp.int32)
    n_tiles = (2 * t) // tm + N_EXPERTS
    tile_start = jnp.arange(n_tiles, dtype=jnp.int32) * tm
    tile_eid = jnp.minimum(jnp.sum((ends[None, :] <= tile_start[:, None]).astype(jnp.int32), axis=1),
                           N_EXPERTS - 1)
    n_valid = (ends[-1:] // tm).astype(jnp.int32)
    xs = _dispatch(h, pos1, pos2, n_tiles * tm)
    ys = _ffn_grouped(xs, tile_eid, n_valid, wgu, wd)
    return _combine(h, ys, pos1, pos2, info[:, 2:3], info[:, 3:4], g, b)


def _rope_tables(seq):
    tpos = jnp.arange(seq)
    row = (tpos // GRID_W).astype(F32)
    col = (tpos % GRID_W).astype(F32)

    def angles(rot_dim):
        n = rot_dim // 4
        inv = ROPE_THETA ** (-jnp.arange(n, dtype=F32) / n)
        ang = jnp.concatenate([row[:, None] * inv, col[:, None] * inv], axis=-1)
        return jnp.cos(ang), jnp.sin(ang)

    cb, sb = angles(HEAD_DIM)
    cos_b = jnp.concatenate([cb, cb, cb, cb], axis=-1)
    sin_b = jnp.concatenate([-sb, sb, -sb, sb], axis=-1)
    cc, sc = angles(MLA_ROPE)
    ones = jnp.ones((seq, MLA_NOPE), F32)
    z64 = jnp.zeros((seq, MLA_NOPE), F32)
    z32 = jnp.zeros((seq, LANES - MLA_NOPE - MLA_ROPE), F32)
    cos_c = jnp.concatenate([ones, cc, cc, z32], axis=-1)
    sin_c = jnp.concatenate([z64, -sc, sc, z32], axis=-1)
    return cos_b, sin_b, cos_c, sin_c


def _mixer_weights(w_in, gq, gk, w_uq, w_ukv, w_proj_b):
    o = IN_OFFS
    nl, d = w_in.shape[:2]
    half = HEAD_DIM // 2
    w_a = jnp.concatenate([w_in[..., o[0]:o[1]] * (HEAD_DIM ** -0.5), w_in[..., o[1]:o[3]]], axis=-1)
    pair_order = np.array([0, 4, 1, 5, 2, 6, 3, 7])
    wq = w_in[..., o[3]:o[4]].reshape(nl, d, GQA_HEADS, HEAD_DIM)[:, :, pair_order]
    wk = w_in[..., o[4]:o[5]].reshape(nl, d, GQA_KV_HEADS, HEAD_DIM)
    flat = lambda w: w.reshape(w.shape[0], w.shape[1], -1)
    w_b = jnp.concatenate([flat(wq), flat(jnp.roll(wq, half, axis=-1)),
                           flat(wk), flat(jnp.roll(wk, half, axis=-1)),
                           w_in[..., o[5]:o[6]]], axis=-1)
    g2 = lambda g: jnp.tile(g, (1, LANES // HEAD_DIM)).reshape(nl, 1, LANES)
    gains_b = (g2(gq), g2(jnp.roll(gq, half, axis=-1)), g2(gk), g2(jnp.roll(gk, half, axis=-1)))
    wkr = w_in[..., o[8]:o[9]]
    pad_kr = lambda w: jnp.concatenate(
        [jnp.zeros((nl, d, MLA_NOPE), F32), w, jnp.zeros((nl, d, LANES - MLA_NOPE - MLA_ROPE), F32)], axis=-1)
    w_c = jnp.concatenate([w_in[..., o[6]:o[8]], pad_kr(wkr), pad_kr(jnp.roll(wkr, MLA_ROPE // 2, axis=-1))],
                          axis=-1)
    uq = w_uq.reshape(nl, MLA_Q_RANK, MLA_HEADS, MLA_NOPE + MLA_ROPE)
    zq = lambda n: jnp.zeros((nl, MLA_Q_RANK, MLA_HEADS, n), F32)
    q_pad = jnp.concatenate([uq, zq(LANES - MLA_NOPE - MLA_ROPE)], axis=-1)
    q_rot = jnp.concatenate([zq(MLA_NOPE), jnp.roll(uq[..., MLA_NOPE:], MLA_ROPE // 2, axis=-1),
                             zq(LANES - MLA_NOPE - MLA_ROPE)], axis=-1)
    wuq = jnp.concatenate([flat(q_pad), flat(q_rot)], axis=-1)
    ukv = w_ukv.reshape(nl, MLA_KV_RANK, MLA_HEADS, MLA_NOPE + MLA_V)
    k_part = jnp.concatenate(
        [ukv[..., :MLA_NOPE], jnp.zeros((nl, MLA_KV_RANK, MLA_HEADS, LANES - MLA_NOPE), F32)], axis=-1)
    wukv = jnp.concatenate([flat(k_part), flat(ukv[..., MLA_NOPE:])], axis=-1)
    wpb = w_proj_b.reshape(nl, GQA_HEADS, HEAD_DIM, -1)[:, pair_order].reshape(nl, B_WIDTH, -1)
    bf = lambda w: w.astype(BF16)
    return bf(w_a), bf(w_b), gains_b, bf(w_c), bf(wuq), bf(wukv), bf(w_in[..., o[9]:]), bf(wpb)


def kernel(x, mem, emb_ln_g, emb_ln_b, w_in, na_rpb, gqa_q_norm, gqa_k_norm, mla_q_norm, mla_kv_norm,
           mla_w_uq, mla_w_ukv, w_proj_a, w_proj_b, w_proj_c, w_out, mem_wq, mem_wkv, mem_wo, ln_g, ln_b,
           ffn_w_gu, ffn_w_down, moe_router, moe_w_gu, moe_w_down):
    batch, seq, d = x.shape
    t = batch * seq
    cos_b, sin_b, cos_c, sin_c = _rope_tables(seq)
    seg = np.kron(np.eye(LANES // HEAD_DIM), np.full((HEAD_DIM, HEAD_DIM), 1.0 / HEAD_DIM))
    seg = jnp.asarray(seg, BF16)
    mem2 = mem.reshape(-1, d)
    vec = lambda v: v.reshape(1, -1)
    bf = lambda w: w.astype(BF16)

    w_a, w_b, gains_b, w_c, wuq, wukv, w_g, wpb = _mixer_weights(
        w_in, gqa_q_norm, gqa_k_norm, mla_w_uq, mla_w_ukv, w_proj_b)
    na_bias = _na_bias_tiles(na_rpb, seq // GRID_W)
    wpa, wpc, wo, mwq, mwkv, mwo = (bf(w) for w in (w_proj_a, w_proj_c, w_out, mem_wq, mem_wkv, mem_wo))
    ffn_gu, ffn_d, moe_gu, moe_d = (bf(w) for w in (ffn_w_gu, ffn_w_down, moe_w_gu, moe_w_down))

    h = _layer_norm(x.reshape(t, d), emb_ln_g, emb_ln_b)
    for l in range(DEPTH):
        qkv_a = _proj(h, w_a[l], name="proj_a")
        ya = _neighborhood_attention(qkv_a, na_bias[l], batch, seq)
        qb, kb, vb = _b_prologue(h, w_b[l], *(g[l] for g in gains_b), cos_b, sin_b, seg, seq)
        yb = _paired_attention(qb, kb, vb, batch, seq, shared=True)
        qc, kc, vc = _c_prologue(h, w_c[l], vec(mla_q_norm[l]), vec(mla_kv_norm[l]), wuq[l], wukv[l],
                                 cos_c, sin_c, seq)
        yc = _paired_attention(qc, kc, vc, batch, seq, shared=False)
        gates = _proj(h, w_g[l], act="sigmoid", name="proj_gates")
        h = _merge_out(ya, yb, yc, gates, h, wpa[l], wpb[l], wpc[l], wo[l], vec(ln_g[l, 0]), vec(ln_b[l, 0]))
        kv = _proj(mem2, mwkv[l], tm=256, name="proj_mem_kv")
        h = _cross_attention(h, kv, mwq[l], mwo[l], vec(ln_g[l, 1]), vec(ln_b[l, 1]), batch, seq)
        if l % 2 == 0:
            h = _ffn_dense(h, ffn_gu[l // 2:l // 2 + 1], ffn_d[l // 2:l // 2 + 1],
                           vec(ln_g[l, 2]), vec(ln_b[l, 2]))
        else:
            h = _moe_layer(h, moe_router[l // 2], moe_gu[l // 2], moe_d[l // 2],
                           vec(ln_g[l, 2]), vec(ln_b[l, 2]))
    return h.reshape(batch, seq, d)
```
